```python
import math
import jax
import jax.numpy as jnp
from jax import lax
import numpy as np

D_MODEL = 1024
BATCH = 4
SEQ = 4096
DEPTH = 4
DEC_BATCH = 128
DEC_SEQ = 8
PAST_LEN = 2048
PAGE_SIZE = 128

POOL_WINDOWS = (2, 4, 8, 16)
POOL_GROUPS = 4
POOL_GROUP_DIM = D_MODEL // 16
POOL_DIM = POOL_GROUPS * POOL_GROUP_DIM
POOL_BUF = max(POOL_WINDOWS) - 1
CHUNK = 128
SGU_GROUPS = 4
SGU_GROUP_DIM = D_MODEL // 16
SGU_DIM = SGU_GROUPS * SGU_GROUP_DIM
SB_HEAD_DIM = 64
SB_HEADS = D_MODEL // 128
SB_DIM = SB_HEADS * SB_HEAD_DIM
Q_BLOCK = 128
SB_BIAS_INIT = -6.0
N_BRANCH = 3
IN_DIM = POOL_DIM + 2 * SGU_DIM + 3 * SB_DIM + N_BRANCH * D_MODEL
D_FF = -(-8 * D_MODEL // (3 * 256)) * 256
NORM_EPS = 1e-6

kernel_name = 'pool_sgu_stickbreak_gated_hybrid_step'


def _rms(x, g):
    xf = x.astype(jnp.float32)
    y = xf * lax.rsqrt(jnp.mean(xf * xf, axis=-1, keepdims=True) + NORM_EPS)
    return y.astype(x.dtype) * g


def _pool_mixer(a, prev, pos, w_pool, scale):
    bsz, t, _ = a.shape
    ext = jnp.concatenate([prev.astype(a.dtype), a], axis=1).astype(jnp.float32)
    csum = jnp.pad(jnp.cumsum(ext, axis=1), ((0, 0), (1, 0), (0, 0)))
    means = []
    for g, w in enumerate(POOL_WINDOWS):
        ch = slice(g * POOL_GROUP_DIM, (g + 1) * POOL_GROUP_DIM)
        s = csum[:, POOL_BUF + 1:POOL_BUF + 1 + t, ch] - csum[:, POOL_BUF + 1 - w:POOL_BUF + 1 - w + t, ch]
        cnt = jnp.minimum(pos + 1, w).astype(jnp.float32)
        means.append(s / cnt[None, :, None])
    d = jnp.concatenate(means, axis=-1) - ext[:, POOL_BUF:]
    d = d.astype(a.dtype).reshape(bsz, t, POOL_GROUPS, POOL_GROUP_DIM)
    y = jnp.einsum('btge,gef->btgf', d, w_pool).reshape(bsz, t, POOL_DIM) * scale
    return y, ext[:, -POOL_BUF:].astype(a.dtype)


def _chunk_mix(v, w_s, b_s):
    bsz, t, _ = v.shape
    pad = (-t) % CHUNK
    vp = jnp.pad(v, ((0, 0), (0, pad), (0, 0)))
    n = (t + pad) // CHUNK
    vr = vp.reshape(bsz, n, CHUNK, SGU_GROUPS, SGU_GROUP_DIM)
    causal = jnp.tril(jnp.ones((CHUNK, CHUNK), dtype=bool))
    w = jnp.where(causal[None], w_s, jnp.zeros((), w_s.dtype))
    o = jnp.einsum('gts,bnsge->bntge', w, vr) + jnp.transpose(b_s)[None, None, :, :, None]
    return o.reshape(bsz, n * CHUNK, SGU_DIM)[:, :t]


def _stick_breaking(q, k, v, q_pos, k_pos, bias):
    bsz, tq, h, dh = q.shape
    blk = min(Q_BLOCK, tq)
    nblk = -(-tq // blk)
    pad = nblk * blk - tq
    qp = jnp.pad(q, ((0, 0), (0, pad), (0, 0), (0, 0))).reshape(bsz, nblk, blk, h, dh).transpose(1, 0, 2, 3, 4)
    pp = jnp.pad(q_pos, (0, pad), constant_values=-1).reshape(nblk, blk)
    scale = dh ** -0.5
    bias_f = bias.astype(jnp.float32)[None, :, None, None]

    def block(args):
        qb, pb = args
        z = jnp.einsum('bqhd,bkhd->bhqk', qb, k).astype(jnp.float32) * scale + bias_f
        mask = (k_pos[None, :] < pb[:, None])[None, None]
        log_beta = jax.nn.log_sigmoid(z)
        log_rem = jnp.where(mask, jax.nn.log_sigmoid(-z), 0.0)
        between = lax.cumsum(log_rem, axis=3, reverse=True) - log_rem
        att = jnp.where(mask, jnp.exp(log_beta + between), 0.0)
        return jnp.einsum('bhqk,bkhd->bqhd', att.astype(v.dtype), v)

    o = lax.map(block, (qp, pp))
    return o.transpose(1, 0, 2, 3, 4).reshape(bsz, nblk * blk, h, dh)[:, :tq]


def _layer(x, pos, k_past, v_past, kpos_past, pool_prev, norm1, w_in, w_pool, pool_scale, w_s, b_s,
           q_gain, k_gain, sb_bias, w_pa, w_pb, w_pc, w_o, norm2, w_gate, w_up, w_down):
    bsz, t, _ = x.shape
    h = _rms(x, norm1)
    proj = jnp.einsum('btd,de->bte', h, w_in)
    sizes = (POOL_DIM, SGU_DIM, SGU_DIM, SB_DIM, SB_DIM, SB_DIM, D_MODEL, D_MODEL, D_MODEL)
    idx = [int(i) for i in np.cumsum(sizes)[:-1]]
    a_in, u, vb, q, k, vc, ga, gb, gc = jnp.split(proj, idx, axis=-1)
    a_out, new_pool = _pool_mixer(a_in, pool_prev, pos, w_pool, pool_scale)
    b_out = u * _chunk_mix(vb, w_s, b_s)
    q = _rms(q.reshape(bsz, t, SB_HEADS, SB_HEAD_DIM), q_gain)
    k = _rms(k.reshape(bsz, t, SB_HEADS, SB_HEAD_DIM), k_gain)
    vc = vc.reshape(bsz, t, SB_HEADS, SB_HEAD_DIM)
    if k_past is None:
        k_all, v_all, kpos = k, vc, pos
    else:
        k_all = jnp.concatenate([k_past.astype(k.dtype), k], axis=1)
        v_all = jnp.concatenate([v_past.astype(vc.dtype), vc], axis=1)
        kpos = jnp.concatenate([kpos_past, pos], axis=0)
    c_out = _stick_breaking(q, k_all, v_all, pos, kpos, sb_bias).reshape(bsz, t, SB_DIM)
    merged = (jax.nn.sigmoid(ga) * (a_out @ w_pa)
              + jax.nn.sigmoid(gb) * (b_out @ w_pb)
              + jax.nn.sigmoid(gc) * (c_out @ w_pc))
    x = x + merged @ w_o
    h2 = _rms(x, norm2)
    x = x + (jax.nn.silu(h2 @ w_gate) * (h2 @ w_up)) @ w_down
    return x, k, vc, new_pool, vb


def setup_inputs(seed: int = 0) -> dict:
    key = jax.random.key(seed)
    ks = jax.random.split(key, 24)
    f32 = jnp.float32
    n_pages = PAST_LEN // PAGE_SIZE
    n_pool = (5 * DEC_BATCH * n_pages + 3) // 4

    def nrm(k, shape, fan_in):
        return jax.random.normal(k, shape, f32) * (fan_in ** -0.5)

    def gain(k, shape, s):
        return 1.0 + s * jax.random.normal(k, shape, f32)

    x_prompt = jax.random.normal(ks[0], (BATCH, SEQ, D_MODEL), f32)
    x_sample = jax.random.normal(ks[1], (DEC_BATCH, DEC_SEQ, D_MODEL), f32)
    cache_k = jax.random.normal(ks[2], (DEPTH, n_pool, PAGE_SIZE, SB_HEADS, SB_HEAD_DIM), f32)
    cache_v = jax.random.normal(ks[3], (DEPTH, n_pool, PAGE_SIZE, SB_HEADS, SB_HEAD_DIM), f32)
    state_pool = jax.random.normal(ks[4], (DEPTH, DEC_BATCH, POOL_BUF, POOL_DIM), f32)
    page_table = jax.random.permutation(ks[5], n_pool)[:DEC_BATCH * n_pages].reshape(DEC_BATCH, n_pages).astype(jnp.int32)
    return {
        'x_prompt': x_prompt,
        'x_sample': x_sample,
        'cache_k': cache_k,
        'cache_v': cache_v,
        'state_pool': state_pool,
        'page_table': page_table,
        'norm1': gain(ks[6], (DEPTH, D_MODEL), 0.01),
        'w_in': nrm(ks[7], (DEPTH, D_MODEL, IN_DIM), D_MODEL),
        'w_pool': nrm(ks[8], (DEPTH, POOL_GROUPS, POOL_GROUP_DIM, POOL_GROUP_DIM), POOL_GROUP_DIM),
        'pool_scale': gain(ks[9], (DEPTH, POOL_DIM), 0.1),
        'w_s': nrm(ks[10], (DEPTH, SGU_GROUPS, CHUNK, CHUNK), CHUNK),
        'b_s': gain(ks[11], (DEPTH, SGU_GROUPS, CHUNK), 0.01),
        'q_gain': gain(ks[12], (DEPTH, SB_HEAD_DIM), 0.01),
        'k_gain': gain(ks[13], (DEPTH, SB_HEAD_DIM), 0.01),
        'sb_bias': SB_BIAS_INIT + 0.1 * jax.random.normal(ks[22], (DEPTH, SB_HEADS), f32),
        'w_pa': nrm(ks[14], (DEPTH, POOL_DIM, D_MODEL), POOL_DIM),
        'w_pb': nrm(ks[15], (DEPTH, SGU_DIM, D_MODEL), SGU_DIM),
        'w_pc': nrm(ks[16], (DEPTH, SB_DIM, D_MODEL), SB_DIM),
        'w_o': nrm(ks[17], (DEPTH, D_MODEL, D_MODEL), D_MODEL),
        'norm2': gain(ks[18], (DEPTH, D_MODEL), 0.01),
        'w_gate': nrm(ks[19], (DEPTH, D_MODEL, D_FF), D_MODEL),
        'w_up': nrm(ks[20], (DEPTH, D_MODEL, D_FF), D_MODEL),
        'w_down': nrm(ks[21], (DEPTH, D_FF, D_MODEL), D_FF),
    }


def reference(x_prompt, x_sample, cache_k, cache_v, state_pool, page_table, norm1, w_in, w_pool,
              pool_scale, w_s, b_s, q_gain, k_gain, sb_bias, w_pa, w_pb, w_pc, w_o, norm2, w_gate, w_up, w_down):
    n_seq, n_pages = page_table.shape
    past_len = n_pages * cache_k.shape[2]
    pos_p = jnp.arange(x_prompt.shape[1], dtype=jnp.int32)
    pos_s = past_len + jnp.arange(x_sample.shape[1], dtype=jnp.int32)
    kpos_past = jnp.arange(past_len, dtype=jnp.int32)
    pool0 = jnp.zeros((x_prompt.shape[0], POOL_BUF, POOL_DIM), x_prompt.dtype)
    xp, xs = x_prompt, x_sample
    kp_l, vp_l, poolp_l, ks_l, vs_l, pools_l, chv_l = [], [], [], [], [], [], []
    for l in range(DEPTH):
        w = (norm1[l], w_in[l], w_pool[l], pool_scale[l], w_s[l], b_s[l], q_gain[l], k_gain[l], sb_bias[l],
             w_pa[l], w_pb[l], w_pc[l], w_o[l], norm2[l], w_gate[l], w_up[l], w_down[l])
        xp, kp, vp, poolp, _ = _layer(xp, pos_p, None, None, None, pool0, *w)
        k_past = jnp.take(cache_k[l], page_table, axis=0).reshape(n_seq, past_len, SB_HEADS, SB_HEAD_DIM)
        v_past = jnp.take(cache_v[l], page_table, axis=0).reshape(n_seq, past_len, SB_HEADS, SB_HEAD_DIM)
        xs, ks_, vs_, pools, chv = _layer(xs, pos_s, k_past, v_past, kpos_past, state_pool[l], *w)
        kp_l.append(kp)
        vp_l.append(vp)
        poolp_l.append(poolp)
        ks_l.append(ks_)
        vs_l.append(vs_)
        pools_l.append(pools)
        chv_l.append(chv)
    return (xp, xs, jnp.stack(kp_l), jnp.stack(vp_l), jnp.stack(poolp_l),
            jnp.stack(ks_l), jnp.stack(vs_l), jnp.stack(pools_l), jnp.stack(chv_l))
```

```python
import functools

import jax
import jax.numpy as jnp
from jax import lax
from jax.experimental import pallas as pl
from jax.experimental.pallas import tpu as pltpu

F32 = jnp.float32
BF16 = jnp.bfloat16

NORM_EPS = 1e-6
POOL_WINDOWS = (2, 4, 8, 16)
POOL_BUF = 15
GROUP_DIM = 64
N_GROUPS = 4
MIX_DIM = N_GROUPS * GROUP_DIM
CHUNK = 128
HEAD_DIM = 64
LANES = 128
VMEM_LIMIT_BYTES = 56 * 1024 * 1024


def _cparams(n_grid_dims):
    return pltpu.CompilerParams(
        dimension_semantics=("arbitrary",) * n_grid_dims,
        vmem_limit_bytes=VMEM_LIMIT_BYTES)


def _dot(a, b):
    return jnp.dot(a, b, preferred_element_type=F32)


def _dot_nt(a, b):
    return lax.dot_general(a, b, (((1,), (1,)), ((), ())), preferred_element_type=F32)


def _rms_rows(x, gain):
    ms = jnp.mean(x * x, axis=-1, keepdims=True)
    return (x * lax.rsqrt(ms + NORM_EPS)) * gain


def _sigmoid(x):
    return 1.0 / (1.0 + jnp.exp(-x))


def _in_proj_body(x_ref, n1_ref, w_ref, qg_ref, kg_ref, hmean_ref,
                  a_ref, u_ref, vb_ref, q_ref, k32_ref, kb_ref, v32_ref, vbf_ref, g_ref,
                  *, sb_dim, d_model):
    hb = _rms_rows(x_ref[...], n1_ref[...]).astype(BF16)

    def proj(c0, width):
        return _dot(hb, w_ref[:, c0:c0 + width])

    def head_rms(y, gain):
        sq = y * y
        hi = sq.astype(BF16)
        lo = (sq - hi.astype(F32)).astype(BF16)
        ms = _dot(hi, hmean_ref[...]) + _dot(lo, hmean_ref[...])
        return (y * lax.rsqrt(ms + NORM_EPS)) * gain

    c = 0
    a_ref[...] = proj(c, MIX_DIM)
    c += MIX_DIM
    u_ref[...] = proj(c, MIX_DIM)
    c += MIX_DIM
    vb_ref[...] = proj(c, MIX_DIM)
    c += MIX_DIM
    q_ref[...] = head_rms(proj(c, sb_dim), qg_ref[...]).astype(BF16)
    c += sb_dim
    k = head_rms(proj(c, sb_dim), kg_ref[...])
    k32_ref[...] = k
    kb_ref[...] = k.astype(BF16)
    c += sb_dim
    v = proj(c, sb_dim)
    v32_ref[...] = v
    vbf_ref[...] = v.astype(BF16)
    c += sb_dim
    for j in range(3):
        g_ref[:, j * d_model:(j + 1) * d_model] = _sigmoid(proj(c, d_model)).astype(BF16)
        c += d_model


def _in_proj(x, norm1, w_in, q_gain_s, k_gain_t, hmean, *, tm):
    n, d_model = x.shape
    in_dim = w_in.shape[1]
    sb_dim = hmean.shape[0]
    row = lambda width: pl.BlockSpec((tm, width), lambda i: (i, 0))
    full = lambda arr: pl.BlockSpec(arr.shape, lambda i: (0,) * arr.ndim)
    out_shapes = (
        jax.ShapeDtypeStruct((n, MIX_DIM), F32),
        jax.ShapeDtypeStruct((n, MIX_DIM), F32),
        jax.ShapeDtypeStruct((n, MIX_DIM), F32),
        jax.ShapeDtypeStruct((n, sb_dim), BF16),
        jax.ShapeDtypeStruct((n, sb_dim), F32),
        jax.ShapeDtypeStruct((n, sb_dim), BF16),
        jax.ShapeDtypeStruct((n, sb_dim), F32),
        jax.ShapeDtypeStruct((n, sb_dim), BF16),
        jax.ShapeDtypeStruct((n, 3 * d_model), BF16),
    )
    return pl.pallas_call(
        functools.partial(_in_proj_body, sb_dim=sb_dim, d_model=d_model),
        grid=(n // tm,),
        in_specs=[row(d_model), full(norm1), full(w_in), full(q_gain_s), full(k_gain_t), full(hmean)],
        out_specs=[row(s.shape[1]) for s in out_shapes],
        out_shape=out_shapes,
        compiler_params=_cparams(1),
        name="in_proj",
    )(x, norm1, w_in, q_gain_s, k_gain_t, hmean)


def _group_of_lane(shape):
    return lax.broadcasted_iota(jnp.int32, shape, len(shape) - 1) // GROUP_DIM


def _mix_prompt_body(a_ref, halo_ref, u_ref, vb_ref, ws_ref, bsf_ref, wpool_ref, pscale_ref,
                     ao_ref, bo_ref, e_ref, s2_ref, s4_ref, s8_ref):
    c = pl.program_id(1)
    a = a_ref[...]
    e_ref[0:16, :] = jnp.zeros((16, MIX_DIM), F32)
    e_ref[16:32, :] = jnp.where(c > 0, halo_ref[...], 0.0)
    e_ref[32:160, :] = a
    s2_ref[8:160, :] = e_ref[8:160, :] + e_ref[7:159, :]
    s4_ref[16:160, :] = s2_ref[16:160, :] + s2_ref[14:158, :]
    s8_ref[24:160, :] = s4_ref[24:160, :] + s4_ref[20:156, :]
    s16 = s8_ref[32:160, :] + s8_ref[24:152, :]
    grp = _group_of_lane((CHUNK, MIX_DIM))
    row = lax.broadcasted_iota(jnp.int32, (CHUNK, MIX_DIM), 0)
    s = jnp.where(grp == 0, s2_ref[32:160, :],
                  jnp.where(grp == 1, s4_ref[32:160, :],
                            jnp.where(grp == 2, s8_ref[32:160, :], s16)))
    window = jnp.left_shift(2, grp)
    cnt = jnp.minimum(c * CHUNK + row + 1, window).astype(F32)
    d = s / cnt - a
    ao_ref[...] = (_dot(d.astype(BF16), wpool_ref[...]) * pscale_ref[...]).astype(BF16)
    vbb = vb_ref[...].astype(BF16)
    r128 = lax.broadcasted_iota(jnp.int32, (CHUNK, CHUNK), 0)
    c128 = lax.broadcasted_iota(jnp.int32, (CHUNK, CHUNK), 1)
    o = jnp.zeros((CHUNK, MIX_DIM), F32)
    for g in range(N_GROUPS):
        wg = jnp.where(c128 <= r128, ws_ref[g], jnp.zeros((), BF16))
        o = jnp.where(grp == g, _dot(wg, vbb), o)
    bo_ref[...] = (u_ref[...] * (o + bsf_ref[...])).astype(BF16)


def _mix_prompt(a, u, vb, ws, bsf, wpool_bd, pscale):
    bsz, t, _ = a.shape
    blk = pl.BlockSpec((None, CHUNK, MIX_DIM), lambda b, c: (b, c, 0))
    halo = pl.BlockSpec((None, 16, MIX_DIM), lambda b, c: (b, jnp.maximum(c * (CHUNK // 16) - 1, 0), 0))
    full = lambda arr: pl.BlockSpec(arr.shape, lambda b, c: (0,) * arr.ndim)
    out = jax.ShapeDtypeStruct((bsz, t, MIX_DIM), BF16)
    return pl.pallas_call(
        _mix_prompt_body,
        grid=(bsz, t // CHUNK),
        in_specs=[blk, halo, blk, blk, full(ws), full(bsf), full(wpool_bd), full(pscale)],
        out_specs=[blk, blk],
        out_shape=(out, out),
        scratch_shapes=[pltpu.VMEM((160, MIX_DIM), F32)] * 4,
        compiler_params=_cparams(2),
        name="mix_prompt",
    )(a, a, u, vb, ws, bsf, wpool_bd, pscale)


def _mix_sample_body(a_ref, prev_ref, u_ref, vb_ref, wrow_ref, bsf_ref, wpool_ref, pscale_ref,
                     ao_ref, bo_ref, np_ref, *, dec_seq, first_pos):
    nseq = a_ref.shape[1]
    grp = _group_of_lane((nseq, MIX_DIM))
    e = [prev_ref[j] for j in range(POOL_BUF)] + [a_ref[t] for t in range(dec_seq)]
    n_e = len(e)
    sums = []
    prev_level, span = e, 1
    for _ in POOL_WINDOWS:
        level = [None] * n_e
        for j in range(n_e):
            if j - span >= 0 and prev_level[j] is not None and prev_level[j - span] is not None:
                level[j] = prev_level[j] + prev_level[j - span]
        sums.append(level)
        prev_level, span = level, span * 2
    for t in range(dec_seq):
        j = POOL_BUF + t
        mean = None
        for k, w in enumerate(POOL_WINDOWS):
            m = sums[k][j] / float(min(first_pos + t + 1, w))
            mean = m if mean is None else jnp.where(grp == k, m, mean)
        d = mean - e[j]
        ao_ref[t] = (_dot(d.astype(BF16), wpool_ref[...]) * pscale_ref[...]).astype(BF16)
        o = bsf_ref[t:t + 1, :]
        for s in range(t + 1):
            o = o + wrow_ref[t, s:s + 1, :] * vb_ref[s]
        bo_ref[t] = (u_ref[t] * o).astype(BF16)
    for j in range(POOL_BUF):
        np_ref[j] = e[n_e - POOL_BUF + j]


def _mix_sample(a, prev, u, vb, wrow, bsf, wpool_bd, pscale, *, first_pos):
    dec_seq, nseq, _ = a.shape
    full = lambda arr: pl.BlockSpec(arr.shape, lambda i: (0,) * arr.ndim)
    args = (a, prev, u, vb, wrow, bsf, wpool_bd, pscale)
    out_shapes = (jax.ShapeDtypeStruct((dec_seq, nseq, MIX_DIM), BF16),
                  jax.ShapeDtypeStruct((dec_seq, nseq, MIX_DIM), BF16),
                  jax.ShapeDtypeStruct((POOL_BUF, nseq, MIX_DIM), F32))
    return pl.pallas_call(
        functools.partial(_mix_sample_body, dec_seq=dec_seq, first_pos=first_pos),
        grid=(1,),
        in_specs=[full(x) for x in args],
        out_specs=[full(s) for s in out_shapes],
        out_shape=out_shapes,
        compiler_params=_cparams(1),
        name="mix_sample",
    )(*args)


def _sb_tile(z, mask, suffix_ones, vt, carry, acc):
    e = jnp.exp(-jnp.abs(z))
    log_beta = jnp.minimum(z, 0.0) - jnp.log(1.0 + e)
    log_rem = log_beta - z
    if mask is not None:
        log_rem = jnp.where(mask, log_rem, 0.0)
    cs = _dot(log_rem.astype(BF16), suffix_ones)
    att = jnp.exp(log_beta + cs[:, :LANES] + carry)
    if mask is not None:
        att = jnp.where(mask, att, 0.0)
    acc = acc + _dot(att.astype(BF16), vt)
    carry = carry + cs[:, LANES:]
    return carry, acc


def _attn_prompt_body(bias_ref, q_ref, k_ref, v_ref, so_ref, o_ref, *, tq):
    hp = pl.program_id(1)
    i = pl.program_id(2)
    n_diag = tq // LANES
    lane = lax.broadcasted_iota(jnp.int32, (tq, LANES), 1)
    qpos = i * tq + lax.broadcasted_iota(jnp.int32, (tq, LANES), 0)
    q = q_ref[...]
    so = so_ref[...]
    out = None
    for hh in range(2):
        bias = bias_ref[hp * 2 + hh]
        in_head = (lane >= hh * HEAD_DIM) & (lane < (hh + 1) * HEAD_DIM)
        qm = jnp.where(in_head, q, jnp.zeros((), BF16))

        def step(j, carry, acc, masked):
            off = pl.multiple_of(j * LANES, LANES)
            kt = k_ref[pl.ds(off, LANES), :]
            vt = v_ref[pl.ds(off, LANES), :]
            z = _dot_nt(qm, kt) + bias
            mask = (j * LANES + lane < qpos) if masked else None
            return _sb_tile(z, mask, so, vt, carry, acc)

        carry = jnp.zeros((tq, LANES), F32)
        acc = jnp.zeros((tq, LANES), F32)
        first = i * n_diag
        for dj in reversed(range(n_diag)):
            carry, acc = step(first + dj, carry, acc, True)

        def body(t, ca):
            return step(first - 1 - t, ca[0], ca[1], False)

        carry, acc = lax.fori_loop(0, first, body, (carry, acc))
        out = acc if out is None else jnp.where(in_head, acc, out)
    o_ref[...] = out.astype(o_ref.dtype)


def _attn_prompt(q, kb, vb, bias, suffix_ones, *, tq):
    bsz, t, sb_dim = q.shape
    n_pairs = sb_dim // LANES
    qspec = pl.BlockSpec((None, tq, LANES), lambda b, hp, i: (b, i, hp))
    kspec = pl.BlockSpec((None, t, LANES), lambda b, hp, i: (b, 0, hp))
    return pl.pallas_call(
        functools.partial(_attn_prompt_body, tq=tq),
        grid=(bsz, n_pairs, t // tq),
        in_specs=[pl.BlockSpec(memory_space=pltpu.SMEM), qspec, kspec, kspec,
                  pl.BlockSpec(suffix_ones.shape, lambda b, hp, i: (0, 0))],
        out_specs=qspec,
        out_shape=jax.ShapeDtypeStruct((bsz, t, sb_dim), BF16),
        compiler_params=_cparams(3),
        name="attn_prompt",
    )(bias, q, kb, vb, suffix_ones)


def _attn_sample_body(pt_ref, bias_ref, q_ref, kn_ref, vn_ref, ck_ref, cv_ref, so_ref, o_ref,
                      qbd_ref, biasb_ref, carry_ref, acc_ref, kbuf_ref, vbuf_ref,
                      *, n_heads, dec_seq, n_pages):
    del pt_ref
    p = pl.program_id(1)
    rows = n_heads * dec_seq
    sb_dim = n_heads * HEAD_DIM

    @pl.when(p == 0)
    def _init():
        lane = lax.broadcasted_iota(jnp.int32, (dec_seq, sb_dim), 1)
        q = q_ref[...]
        for h in range(n_heads):
            in_head = (lane >= h * HEAD_DIM) & (lane < (h + 1) * HEAD_DIM)
            qbd_ref[h * dec_seq:(h + 1) * dec_seq, :] = jnp.where(in_head, q, 0.0)
            biasb_ref[h * dec_seq:(h + 1) * dec_seq, :] = jnp.full((dec_seq, LANES), bias_ref[h], F32)
        carry_ref[...] = jnp.zeros((rows, LANES), F32)
        acc_ref[...] = jnp.zeros((rows, sb_dim), F32)
        kbuf_ref[...] = jnp.zeros((LANES, sb_dim), F32)
        vbuf_ref[...] = jnp.zeros((LANES, sb_dim), F32)
        kbuf_ref[0:dec_seq, :] = kn_ref[...]
        vbuf_ref[0:dec_seq, :] = vn_ref[...]

    def tile(k32, v32, masked):
        z = _dot_nt(qbd_ref[...].astype(BF16), k32.astype(BF16)) + biasb_ref[...]
        mask = None
        if masked:
            kidx = lax.broadcasted_iota(jnp.int32, (rows, LANES), 1)
            t = lax.broadcasted_iota(jnp.int32, (rows, LANES), 0) % dec_seq
            mask = kidx < t
        carry, acc = _sb_tile(z, mask, so_ref[...], v32.astype(BF16), carry_ref[...], acc_ref[...])
        carry_ref[...] = carry
        acc_ref[...] = acc

    @pl.when(p == 0)
    def _new_keys():
        tile(kbuf_ref[...], vbuf_ref[...], True)

    @pl.when(p > 0)
    def _page():
        tile(ck_ref[...], cv_ref[...], False)

    @pl.when(p == n_pages)
    def _finish():
        lane = lax.broadcasted_iota(jnp.int32, (dec_seq, sb_dim), 1)
        out = jnp.zeros((dec_seq, sb_dim), F32)
        for h in range(n_heads):
            in_head = (lane >= h * HEAD_DIM) & (lane < (h + 1) * HEAD_DIM)
            out = jnp.where(in_head, acc_ref[h * dec_seq:(h + 1) * dec_seq, :], out)
        o_ref[...] = out


def _attn_sample(page_table, bias, q, k_new, v_new, cache_k, cache_v, suffix_ones, *, layer):
    nseq, dec_seq, sb_dim = q.shape
    n_pages = page_table.shape[1]
    page_size = cache_k.shape[2]
    n_heads = sb_dim // HEAD_DIM
    rows = n_heads * dec_seq
    assert page_size == LANES
    seq_spec = pl.BlockSpec((None, dec_seq, sb_dim), lambda n, p, pt: (n, 0, 0))

    def page_map(n, p, pt):
        return (layer, pt[n, n_pages - jnp.maximum(p, 1)], 0, 0)

    page_spec = pl.BlockSpec((None, None, page_size, sb_dim), page_map)
    grid_spec = pltpu.PrefetchScalarGridSpec(
        num_scalar_prefetch=1,
        grid=(nseq, n_pages + 1),
        in_specs=[pl.BlockSpec(memory_space=pltpu.SMEM), seq_spec, seq_spec, seq_spec,
                  page_spec, page_spec,
                  pl.BlockSpec(suffix_ones.shape, lambda n, p, pt: (0, 0))],
        out_specs=seq_spec,
        scratch_shapes=[pltpu.VMEM((rows, sb_dim), F32),
                        pltpu.VMEM((rows, LANES), F32),
                        pltpu.VMEM((rows, LANES), F32),
                        pltpu.VMEM((rows, sb_dim), F32),
                        pltpu.VMEM((LANES, sb_dim), F32),
                        pltpu.VMEM((LANES, sb_dim), F32)],
    )
    return pl.pallas_call(
        functools.partial(_attn_sample_body, n_heads=n_heads, dec_seq=dec_seq, n_pages=n_pages),
        grid_spec=grid_spec,
        out_shape=jax.ShapeDtypeStruct((nseq, dec_seq, sb_dim), F32),
        compiler_params=_cparams(2),
        name="attn_sample",
    )(page_table, bias, q, k_new, v_new, cache_k, cache_v, suffix_ones)


def _merge_body(x_ref, a_ref, b_ref, c_ref, g_ref, wpa_ref, wpb_ref, wpc_ref, wo_ref, n2_ref,
                x1_ref, h2_ref, *, d_model):
    merged = (g_ref[:, 0:d_model].astype(F32) * _dot(a_ref[...], wpa_ref[...])
              + g_ref[:, d_model:2 * d_model].astype(F32) * _dot(b_ref[...], wpb_ref[...])
              + g_ref[:, 2 * d_model:3 * d_model].astype(F32) * _dot(c_ref[...], wpc_ref[...]))
    x1 = x_ref[...] + _dot(merged.astype(BF16), wo_ref[...])
    x1_ref[...] = x1
    h2_ref[...] = _rms_rows(x1, n2_ref[...]).astype(BF16)


def _merge(x, a_out, b_out, c_out, gates, w_pa, w_pb, w_pc, w_o, norm2, *, tm):
    n, d_model = x.shape
    row = lambda arr: pl.BlockSpec((tm, arr.shape[1]), lambda i: (i, 0))
    full = lambda arr: pl.BlockSpec(arr.shape, lambda i: (0,) * arr.ndim)
    rows = (x, a_out, b_out, c_out, gates)
    consts = (w_pa, w_pb, w_pc, w_o, norm2)
    return pl.pallas_call(
        functools.partial(_merge_body, d_model=d_model),
        grid=(n // tm,),
        in_specs=[row(r) for r in rows] + [full(c) for c in consts],
        out_specs=[row(x), row(x)],
        out_shape=(jax.ShapeDtypeStruct((n, d_model), F32), jax.ShapeDtypeStruct((n, d_model), BF16)),
        compiler_params=_cparams(1),
        name="merge",
    )(*rows, *consts)


def _swiglu_body(x_ref, h_ref, wg_ref, wu_ref, wd_ref, o_ref, *, ff_chunk):
    h = h_ref[...]
    d_ff = wg_ref.shape[1]
    acc = x_ref[...]
    for c0 in range(0, d_ff, ff_chunk):
        gate = _dot(h, wg_ref[:, c0:c0 + ff_chunk])
        up = _dot(h, wu_ref[:, c0:c0 + ff_chunk])
        act = (gate * _sigmoid(gate) * up).astype(BF16)
        acc = acc + _dot(act, wd_ref[c0:c0 + ff_chunk, :])
    o_ref[...] = acc


def _swiglu(x, h, w_gate, w_up, w_down, *, tm, ff_chunk):
    n, d_model = x.shape
    row = lambda arr: pl.BlockSpec((tm, arr.shape[1]), lambda i: (i, 0))
    full = lambda arr: pl.BlockSpec(arr.shape, lambda i: (0,) * arr.ndim)
    return pl.pallas_call(
        functools.partial(_swiglu_body, ff_chunk=ff_chunk),
        grid=(n // tm,),
        in_specs=[row(x), row(h), full(w_gate), full(w_up), full(w_down)],
        out_specs=row(x),
        out_shape=jax.ShapeDtypeStruct((n, d_model), F32),
        compiler_params=_cparams(1),
        name="swiglu",
    )(x, h, w_gate, w_up, w_down)


def _row_tile(n, target):
    tm = min(n, target)
    assert n % tm == 0
    return tm


def _ff_chunk(d_ff):
    for c in (512, 256, 128):
        if d_ff % c == 0:
            return c
    return d_ff


def kernel(x_prompt, x_sample, cache_k, cache_v, state_pool, page_table, norm1, w_in, w_pool, pool_scale,
           w_s, b_s, q_gain, k_gain, sb_bias, w_pa, w_pb, w_pc, w_o, norm2, w_gate, w_up, w_down):
    depth = w_in.shape[0]
    bsz, seq, d_model = x_prompt.shape
    nseq, dec_seq, _ = x_sample.shape
    n_pool, page_size, n_heads, head_dim = cache_k.shape[1:]
    n_pages = page_table.shape[1]
    past_len = n_pages * page_size
    sb_dim = n_heads * head_dim
    d_ff = w_gate.shape[2]
    assert head_dim == HEAD_DIM and w_pool.shape[1] == N_GROUPS and w_s.shape[2] == CHUNK
    assert seq % CHUNK == 0 and past_len % CHUNK == 0 and dec_seq <= CHUNK

    lane = jnp.arange(sb_dim)
    hmean = jnp.where((lane[:, None] // HEAD_DIM) == (lane[None, :] // HEAD_DIM), 1.0 / HEAD_DIM, 0.0).astype(BF16)
    kidx = jnp.arange(LANES)
    suffix_ones = jnp.concatenate(
        [(kidx[:, None] > kidx[None, :]).astype(BF16), jnp.ones((LANES, LANES), BF16)], axis=1)
    w_in_b = w_in.astype(BF16)
    w_pa_b, w_pb_b, w_pc_b, w_o_b = (w.astype(BF16) for w in (w_pa, w_pb, w_pc, w_o))
    w_gate_b, w_up_b, w_down_b = (w.astype(BF16) for w in (w_gate, w_up, w_down))
    w_s_b = w_s.astype(BF16)
    grp = jnp.arange(MIX_DIM) // GROUP_DIM
    wpool_bd = jnp.where(grp[None, :, None] == grp[None, None, :],
                         jnp.tile(w_pool.reshape(depth, MIX_DIM, GROUP_DIM), (1, 1, N_GROUPS)), 0.0).astype(BF16)
    bsf = jnp.repeat(jnp.swapaxes(b_s, 1, 2), GROUP_DIM, axis=2)
    wrow = jnp.repeat(jnp.transpose(w_s[:, :, :dec_seq, :dec_seq], (0, 2, 3, 1)), GROUP_DIM, axis=3)
    qg = jnp.tile(q_gain, (1, n_heads)) * (HEAD_DIM ** -0.5)
    kg = jnp.tile(k_gain, (1, n_heads))
    cache_k4 = cache_k.reshape(depth, n_pool, page_size, sb_dim)
    cache_v4 = cache_v.reshape(depth, n_pool, page_size, sb_dim)

    n_p = bsz * seq
    n_s = nseq * dec_seq
    tm_p = _row_tile(n_p, 512)
    tm_s = _row_tile(n_s, 512)
    tq = _row_tile(seq, 128)
    ff_chunk = _ff_chunk(d_ff)

    xp = x_prompt.reshape(n_p, d_model)
    xs = jnp.swapaxes(x_sample, 0, 1).reshape(n_s, d_model)
    outs = [[] for _ in range(7)]
    for l in range(depth):
        lw = dict(norm1=norm1[l][None], w_in=w_in_b[l], qg=qg[l][None], kg=kg[l][None])
        a, u, vb, q, k32, kb, v32, vbf, gates = _in_proj(
            xp, lw["norm1"], lw["w_in"], lw["qg"], lw["kg"], hmean, tm=tm_p)
        a3 = a.reshape(bsz, seq, MIX_DIM)
        a_out, b_out = _mix_prompt(a3, u.reshape(bsz, seq, MIX_DIM), vb.reshape(bsz, seq, MIX_DIM),
                                   w_s_b[l], bsf[l], wpool_bd[l], pool_scale[l][None])
        c_out = _attn_prompt(q.reshape(bsz, seq, sb_dim), kb.reshape(bsz, seq, sb_dim),
                             vbf.reshape(bsz, seq, sb_dim), sb_bias[l], suffix_ones, tq=tq)
        x1, h2 = _merge(xp, a_out.reshape(n_p, MIX_DIM), b_out.reshape(n_p, MIX_DIM), c_out.reshape(n_p, sb_dim),
                        gates, w_pa_b[l], w_pb_b[l], w_pc_b[l], w_o_b[l], norm2[l][None], tm=tm_p)
        xp = _swiglu(x1, h2, w_gate_b[l], w_up_b[l], w_down_b[l], tm=tm_p, ff_chunk=ff_chunk)
        outs[0].append(k32.reshape(bsz, seq, n_heads, head_dim))
        outs[1].append(v32.reshape(bsz, seq, n_heads, head_dim))
        outs[2].append(a3[:, seq - POOL_BUF:, :])
        a, u, vb, q, k32, kb, v32, vbf, gates = _in_proj(
            xs, lw["norm1"], lw["w_in"], lw["qg"], lw["kg"], hmean, tm=tm_s)
        tmaj = lambda arr: arr.reshape(dec_seq, nseq, arr.shape[-1])
        smaj = lambda arr: jnp.swapaxes(tmaj(arr), 0, 1)
        a_out, b_out, new_pool = _mix_sample(
            tmaj(a), jnp.swapaxes(state_pool[l], 0, 1), tmaj(u), tmaj(vb),
            wrow[l], bsf[l], wpool_bd[l], pool_scale[l][None], first_pos=past_len)
        k_new = smaj(k32)
        v_new = smaj(v32)
        c_out = _attn_sample(page_table, sb_bias[l], smaj(q.astype(F32)), k_new, v_new,
                             cache_k4, cache_v4, suffix_ones, layer=l)
        c_out = jnp.swapaxes(c_out, 0, 1).reshape(n_s, sb_dim).astype(BF16)
        x1, h2 = _merge(xs, a_out.reshape(n_s, MIX_DIM), b_out.reshape(n_s, MIX_DIM), c_out,
                        gates, w_pa_b[l], w_pb_b[l], w_pc_b[l], w_o_b[l], norm2[l][None], tm=tm_s)
        xs = _swiglu(x1, h2, w_gate_b[l], w_up_b[l], w_down_b[l], tm=tm_s, ff_chunk=ff_chunk)
        outs[3].append(k_new.reshape(nseq, dec_seq, n_heads, head_dim))
        outs[4].append(v_new.reshape(nseq, dec_seq, n_heads, head_dim))
        outs[5].append(jnp.swapaxes(new_pool, 0, 1))
        outs[6].append(smaj(vb))
    y_prompt = xp.reshape(bsz, seq, d_model)
    y_sample = jnp.swapaxes(xs.reshape(dec_seq, nseq, d_model), 0, 1)
    return (y_prompt, y_sample) + tuple(jnp.stack(o) for o in outs)
```

```python
import functools

import jax
import jax.numpy as jnp
from jax import lax
from jax.experimental import pallas as pl
from jax.experimental.pallas import tpu as pltpu

F32 = jnp.float32
BF16 = jnp.bfloat16

NORM_EPS = 1e-6
POOL_WINDOWS = (2, 4, 8, 16)
POOL_BUF = 15
GROUP_DIM = 64
N_GROUPS = 4
MIX_DIM = N_GROUPS * GROUP_DIM
CHUNK = 128
HEAD_DIM = 64
LANES = 128
LOG2_E = 1.4426950408889634
UNROLL_KEY_TILES = 4
VMEM_LIMIT_BYTES = 56 * 1024 * 1024


def _cparams(n_grid_dims):
    return pltpu.CompilerParams(
        dimension_semantics=("arbitrary",) * n_grid_dims,
        vmem_limit_bytes=VMEM_LIMIT_BYTES)


def _dot(a, b):
    return jnp.dot(a, b, preferred_element_type=F32)


def _dot_nt(a, b):
    return lax.dot_general(a, b, (((1,), (1,)), ((), ())), preferred_element_type=F32)


def _rms_rows(x, gain):
    ms = jnp.mean(x * x, axis=-1, keepdims=True)
    return (x * lax.rsqrt(ms + NORM_EPS)) * gain


def _sigmoid(x):
    return 1.0 / (1.0 + jnp.exp(-x))


def _in_proj_body(x_ref, n1_ref, w_ref, qg_ref, kg_ref, hmean_ref,
                  a_ref, u_ref, vb_ref, q_ref, k32_ref, kb_ref, v32_ref, vbf_ref, g_ref,
                  *, sb_dim, d_model):
    hb = _rms_rows(x_ref[...], n1_ref[...]).astype(BF16)

    def proj(c0, width):
        return _dot(hb, w_ref[:, c0:c0 + width])

    def head_rms(y, gain):
        sq = y * y
        hi = sq.astype(BF16)
        lo = (sq - hi.astype(F32)).astype(BF16)
        ms = _dot(hi, hmean_ref[...]) + _dot(lo, hmean_ref[...])
        return (y * lax.rsqrt(ms + NORM_EPS)) * gain

    c = 0
    a_ref[...] = proj(c, MIX_DIM)
    c += MIX_DIM
    u_ref[...] = proj(c, MIX_DIM)
    c += MIX_DIM
    vb_ref[...] = proj(c, MIX_DIM)
    c += MIX_DIM
    q_ref[...] = head_rms(proj(c, sb_dim), qg_ref[...]).astype(BF16)
    c += sb_dim
    k = head_rms(proj(c, sb_dim), kg_ref[...])
    k32_ref[...] = k
    kb_ref[...] = k.astype(BF16)
    c += sb_dim
    v = proj(c, sb_dim)
    v32_ref[...] = v
    vbf_ref[...] = v.astype(BF16)
    c += sb_dim
    for j in range(3):
        g_ref[:, j * d_model:(j + 1) * d_model] = _sigmoid(proj(c, d_model)).astype(BF16)
        c += d_model


def _in_proj(x, norm1, w_in, q_gain_s, k_gain_t, hmean, *, tm):
    n, d_model = x.shape
    in_dim = w_in.shape[1]
    sb_dim = hmean.shape[0]
    row = lambda width: pl.BlockSpec((tm, width), lambda i: (i, 0))
    full = lambda arr: pl.BlockSpec(arr.shape, lambda i: (0,) * arr.ndim)
    out_shapes = (
        jax.ShapeDtypeStruct((n, MIX_DIM), F32),
        jax.ShapeDtypeStruct((n, MIX_DIM), F32),
        jax.ShapeDtypeStruct((n, MIX_DIM), F32),
        jax.ShapeDtypeStruct((n, sb_dim), BF16),
        jax.ShapeDtypeStruct((n, sb_dim), F32),
        jax.ShapeDtypeStruct((n, sb_dim), BF16),
        jax.ShapeDtypeStruct((n, sb_dim), F32),
        jax.ShapeDtypeStruct((n, sb_dim), BF16),
        jax.ShapeDtypeStruct((n, 3 * d_model), BF16),
    )
    return pl.pallas_call(
        functools.partial(_in_proj_body, sb_dim=sb_dim, d_model=d_model),
        grid=(n // tm,),
        in_specs=[row(d_model), full(norm1), full(w_in), full(q_gain_s), full(k_gain_t), full(hmean)],
        out_specs=[row(s.shape[1]) for s in out_shapes],
        out_shape=out_shapes,
        compiler_params=_cparams(1),
        name="in_proj",
    )(x, norm1, w_in, q_gain_s, k_gain_t, hmean)


def _group_of_lane(shape):
    return lax.broadcasted_iota(jnp.int32, shape, len(shape) - 1) // GROUP_DIM


def _mix_prompt_body(a_ref, halo_ref, u_ref, vb_ref, ws_ref, bsf_ref, wpool_ref, pscale_ref,
                     ao_ref, bo_ref, e_ref, s2_ref, s4_ref, s8_ref):
    c = pl.program_id(1)
    a = a_ref[...]
    e_ref[0:16, :] = jnp.zeros((16, MIX_DIM), F32)
    e_ref[16:32, :] = jnp.where(c > 0, halo_ref[...], 0.0)
    e_ref[32:160, :] = a
    s2_ref[8:160, :] = e_ref[8:160, :] + e_ref[7:159, :]
    s4_ref[16:160, :] = s2_ref[16:160, :] + s2_ref[14:158, :]
    s8_ref[24:160, :] = s4_ref[24:160, :] + s4_ref[20:156, :]
    s16 = s8_ref[32:160, :] + s8_ref[24:152, :]
    grp = _group_of_lane((CHUNK, MIX_DIM))
    row = lax.broadcasted_iota(jnp.int32, (CHUNK, MIX_DIM), 0)
    s = jnp.where(grp == 0, s2_ref[32:160, :],
                  jnp.where(grp == 1, s4_ref[32:160, :],
                            jnp.where(grp == 2, s8_ref[32:160, :], s16)))
    window = jnp.left_shift(2, grp)
    cnt = jnp.minimum(c * CHUNK + row + 1, window).astype(F32)
    d = s / cnt - a
    ao_ref[...] = (_dot(d.astype(BF16), wpool_ref[...]) * pscale_ref[...]).astype(BF16)
    vbb = vb_ref[...].astype(BF16)
    r128 = lax.broadcasted_iota(jnp.int32, (CHUNK, CHUNK), 0)
    c128 = lax.broadcasted_iota(jnp.int32, (CHUNK, CHUNK), 1)
    o = jnp.zeros((CHUNK, MIX_DIM), F32)
    for g in range(N_GROUPS):
        wg = jnp.where(c128 <= r128, ws_ref[g], jnp.zeros((), BF16))
        o = jnp.where(grp == g, _dot(wg, vbb), o)
    bo_ref[...] = (u_ref[...] * (o + bsf_ref[...])).astype(BF16)


def _mix_prompt(a, u, vb, ws, bsf, wpool_bd, pscale):
    bsz, t, _ = a.shape
    blk = pl.BlockSpec((None, CHUNK, MIX_DIM), lambda b, c: (b, c, 0))
    halo = pl.BlockSpec((None, 16, MIX_DIM), lambda b, c: (b, jnp.maximum(c * (CHUNK // 16) - 1, 0), 0))
    full = lambda arr: pl.BlockSpec(arr.shape, lambda b, c: (0,) * arr.ndim)
    out = jax.ShapeDtypeStruct((bsz, t, MIX_DIM), BF16)
    return pl.pallas_call(
        _mix_prompt_body,
        grid=(bsz, t // CHUNK),
        in_specs=[blk, halo, blk, blk, full(ws), full(bsf), full(wpool_bd), full(pscale)],
        out_specs=[blk, blk],
        out_shape=(out, out),
        scratch_shapes=[pltpu.VMEM((160, MIX_DIM), F32)] * 4,
        compiler_params=_cparams(2),
        name="mix_prompt",
    )(a, a, u, vb, ws, bsf, wpool_bd, pscale)


def _mix_sample_body(a_ref, prev_ref, u_ref, vb_ref, wrow_ref, bsf_ref, wpool_ref, pscale_ref,
                     ao_ref, bo_ref, np_ref, *, dec_seq, first_pos):
    nseq = a_ref.shape[1]
    grp = _group_of_lane((nseq, MIX_DIM))
    e = [prev_ref[j] for j in range(POOL_BUF)] + [a_ref[t] for t in range(dec_seq)]
    n_e = len(e)
    sums = []
    prev_level, span = e, 1
    for _ in POOL_WINDOWS:
        level = [None] * n_e
        for j in range(n_e):
            if j - span >= 0 and prev_level[j] is not None and prev_level[j - span] is not None:
                level[j] = prev_level[j] + prev_level[j - span]
        sums.append(level)
        prev_level, span = level, span * 2
    for t in range(dec_seq):
        j = POOL_BUF + t
        mean = None
        for k, w in enumerate(POOL_WINDOWS):
            m = sums[k][j] / float(min(first_pos + t + 1, w))
            mean = m if mean is None else jnp.where(grp == k, m, mean)
        d = mean - e[j]
        ao_ref[t] = (_dot(d.astype(BF16), wpool_ref[...]) * pscale_ref[...]).astype(BF16)
        o = bsf_ref[t:t + 1, :]
        for s in range(t + 1):
            o = o + wrow_ref[t, s:s + 1, :] * vb_ref[s]
        bo_ref[t] = (u_ref[t] * o).astype(BF16)
    for j in range(POOL_BUF):
        np_ref[j] = e[n_e - POOL_BUF + j]


def _mix_sample(a, prev, u, vb, wrow, bsf, wpool_bd, pscale, *, first_pos):
    dec_seq, nseq, _ = a.shape
    full = lambda arr: pl.BlockSpec(arr.shape, lambda i: (0,) * arr.ndim)
    args = (a, prev, u, vb, wrow, bsf, wpool_bd, pscale)
    out_shapes = (jax.ShapeDtypeStruct((dec_seq, nseq, MIX_DIM), BF16),
                  jax.ShapeDtypeStruct((dec_seq, nseq, MIX_DIM), BF16),
                  jax.ShapeDtypeStruct((POOL_BUF, nseq, MIX_DIM), F32))
    return pl.pallas_call(
        functools.partial(_mix_sample_body, dec_seq=dec_seq, first_pos=first_pos),
        grid=(1,),
        in_specs=[full(x) for x in args],
        out_specs=[full(s) for s in out_shapes],
        out_shape=out_shapes,
        compiler_params=_cparams(1),
        name="mix_sample",
    )(*args)


def _sb_logs(z):
    neg_abs = lax.bitcast_convert_type(lax.bitcast_convert_type(z, jnp.uint32) | jnp.uint32(0x80000000), F32)
    log_beta = jnp.minimum(z, 0.0) - jnp.log2(1.0 + jnp.exp2(neg_abs))
    return log_beta, log_beta - z


def _sb_weights(z, mask, suffix_ones, carry):
    log_beta, log_rem = _sb_logs(z)
    if mask is not None:
        log_rem = jnp.where(mask, log_rem, 0.0)
    cs = _dot(log_rem.astype(BF16), suffix_ones)
    att = jnp.exp2(log_beta + cs[:, :LANES] + carry)
    if mask is not None:
        att = jnp.where(mask, att, 0.0)
    return att, carry + cs[:, LANES:]


def _attn_prompt_body(bias_ref, q_ref, k_ref, v_ref, so_ref, o_ref, carry_ref, acc_ref, *, tq):
    hp = pl.program_id(1)
    i = pl.program_id(2)
    n_diag = tq // LANES
    so = so_ref[...]
    first_head = lax.broadcasted_iota(jnp.int32, (LANES, LANES), 1) < HEAD_DIM
    zero = jnp.zeros((), BF16)
    biases = (bias_ref[hp * 2], bias_ref[hp * 2 + 1])
    carry_ref[...] = jnp.zeros(carry_ref.shape, F32)
    acc_ref[...] = jnp.zeros(acc_ref.shape, F32)

    def step(j, row0, masked):
        rows = tq - row0
        off = pl.multiple_of(j * LANES, LANES)
        kt = k_ref[pl.ds(off, LANES), :]
        vt = v_ref[pl.ds(off, LANES), :]
        kbd = jnp.concatenate([jnp.where(first_head, kt, zero), jnp.where(first_head, zero, kt)], axis=0)
        vbd = jnp.concatenate([jnp.where(first_head, vt, zero), jnp.where(first_head, zero, vt)], axis=0)
        z = _dot_nt(q_ref[row0:tq, :], kbd)
        mask = None
        if masked:
            kpos = j * LANES + lax.broadcasted_iota(jnp.int32, (rows, LANES), 1)
            qpos = i * tq + row0 + lax.broadcasted_iota(jnp.int32, (rows, LANES), 0)
            mask = kpos < qpos
        atts = []
        for hh in range(2):
            att, carry = _sb_weights(z[:, hh * LANES:(hh + 1) * LANES] + biases[hh], mask, so,
                                     carry_ref[hh, row0:tq, :])
            carry_ref[hh, row0:tq, :] = carry
            atts.append(att.astype(BF16))
        acc_ref[row0:tq, :] += _dot(jnp.concatenate(atts, axis=1), vbd)

    first = i * n_diag
    for dj in reversed(range(n_diag)):
        step(first + dj, dj * LANES, True)

    def body(t, c):
        for dj in range(UNROLL_KEY_TILES):
            step(first - 1 - dj - UNROLL_KEY_TILES * t, 0, False)
        return c

    assert n_diag % UNROLL_KEY_TILES == 0
    lax.fori_loop(0, first // UNROLL_KEY_TILES, body, 0)
    o_ref[...] = acc_ref[...].astype(o_ref.dtype)


def _attn_prompt(q, kb, vb, bias, suffix_ones, *, tq):
    bsz, t, sb_dim = q.shape
    n_pairs = sb_dim // LANES
    qspec = pl.BlockSpec((None, tq, LANES), lambda b, hp, i: (b, i, hp))
    kspec = pl.BlockSpec((None, t, LANES), lambda b, hp, i: (b, 0, hp))
    return pl.pallas_call(
        functools.partial(_attn_prompt_body, tq=tq),
        grid=(bsz, n_pairs, t // tq),
        in_specs=[pl.BlockSpec(memory_space=pltpu.SMEM), qspec, kspec, kspec,
                  pl.BlockSpec(suffix_ones.shape, lambda b, hp, i: (0, 0))],
        out_specs=qspec,
        out_shape=jax.ShapeDtypeStruct((bsz, t, sb_dim), BF16),
        scratch_shapes=[pltpu.VMEM((2, tq, LANES), F32), pltpu.VMEM((tq, LANES), F32)],
        compiler_params=_cparams(3),
        name="attn_prompt",
    )(bias, q, kb, vb, suffix_ones)


def _attn_sample_body(pt_ref, bias_ref, q_ref, kn_ref, vn_ref, *rest, n_heads, dec_seq, n_pages):
    del pt_ref
    kpage_refs = rest[:n_pages]
    vpage_refs = rest[n_pages:2 * n_pages]
    so_ref, o_ref = rest[2 * n_pages:]
    rows = n_heads * dec_seq
    sb_dim = n_heads * HEAD_DIM
    lane = lax.broadcasted_iota(jnp.int32, (dec_seq, sb_dim), 1)
    in_head = [(lane >= h * HEAD_DIM) & (lane < (h + 1) * HEAD_DIM) for h in range(n_heads)]
    q = q_ref[...]
    qbd = jnp.concatenate([jnp.where(in_head[h], q, 0.0) for h in range(n_heads)], axis=0).astype(BF16)
    biasb = jnp.concatenate([jnp.full((dec_seq, LANES), bias_ref[h], F32) for h in range(n_heads)], axis=0)
    pad = jnp.zeros((LANES - dec_seq, sb_dim), F32)
    kn = jnp.concatenate([kn_ref[...], pad], axis=0).astype(BF16)
    vn = jnp.concatenate([vn_ref[...], pad], axis=0).astype(BF16)
    kidx = lax.broadcasted_iota(jnp.int32, (rows, LANES), 1)
    trow = lax.broadcasted_iota(jnp.int32, (rows, LANES), 0) % dec_seq
    new_mask = kidx < trow
    pages = list(reversed(range(n_pages)))
    z = jnp.concatenate([_dot_nt(qbd, kn) + biasb]
                        + [_dot(qbd, kpage_refs[pg][...].astype(BF16)) + biasb for pg in pages], axis=0)
    log_beta, log_rem = _sb_logs(z)
    log_rem = jnp.concatenate([jnp.where(new_mask, log_rem[:rows], 0.0), log_rem[rows:]], axis=0)
    cs = _dot(log_rem.astype(BF16), so_ref[...])
    carries = [jnp.zeros((rows, LANES), F32)]
    for p in range(n_pages):
        carries.append(carries[-1] + cs[p * rows:(p + 1) * rows, LANES:])
    att = jnp.exp2(log_beta + cs[:, :LANES] + jnp.concatenate(carries, axis=0))
    acc = _dot(jnp.where(new_mask, att[:rows], 0.0).astype(BF16), vn)
    for p, pg in enumerate(pages):
        acc = acc + _dot_nt(att[(p + 1) * rows:(p + 2) * rows].astype(BF16), vpage_refs[pg][...].astype(BF16))
    out = jnp.zeros((dec_seq, sb_dim), F32)
    for h in range(n_heads):
        out = jnp.where(in_head[h], acc[h * dec_seq:(h + 1) * dec_seq, :], out)
    o_ref[...] = out


def _attn_sample(page_table, bias, q, k_new, v_new, cache_kt, cache_vt, suffix_ones, *, layer):
    nseq, dec_seq, sb_dim = q.shape
    n_heads = sb_dim // HEAD_DIM
    n_pages = page_table.shape[1]
    page_size = cache_kt.shape[-1]
    assert page_size == LANES
    seq_spec = pl.BlockSpec((None, dec_seq, sb_dim), lambda n, pt: (n, 0, 0))

    def page_spec(pg):
        return pl.BlockSpec((None, None, sb_dim, page_size), lambda n, pt: (layer, pt[n, pg], 0, 0))

    page_specs = [page_spec(pg) for pg in range(n_pages)]
    grid_spec = pltpu.PrefetchScalarGridSpec(
        num_scalar_prefetch=1,
        grid=(nseq,),
        in_specs=[pl.BlockSpec(memory_space=pltpu.SMEM), seq_spec, seq_spec, seq_spec]
        + page_specs + page_specs
        + [pl.BlockSpec(suffix_ones.shape, lambda n, pt: (0, 0))],
        out_specs=seq_spec,
    )
    return pl.pallas_call(
        functools.partial(_attn_sample_body, n_heads=n_heads, dec_seq=dec_seq, n_pages=n_pages),
        grid_spec=grid_spec,
        out_shape=jax.ShapeDtypeStruct((nseq, dec_seq, sb_dim), F32),
        compiler_params=_cparams(1),
        name="attn_sample",
    )(page_table, bias, q, k_new, v_new, *([cache_kt] * n_pages), *([cache_vt] * n_pages), suffix_ones)


def _merge_body(x_ref, a_ref, b_ref, c_ref, g_ref, wpa_ref, wpb_ref, wpc_ref, wo_ref, n2_ref,
                x1_ref, h2_ref, *, d_model):
    merged = (g_ref[:, 0:d_model].astype(F32) * _dot(a_ref[...], wpa_ref[...])
              + g_ref[:, d_model:2 * d_model].astype(F32) * _dot(b_ref[...], wpb_ref[...])
              + g_ref[:, 2 * d_model:3 * d_model].astype(F32) * _dot(c_ref[...], wpc_ref[...]))
    x1 = x_ref[...] + _dot(merged.astype(BF16), wo_ref[...])
    x1_ref[...] = x1
    h2_ref[...] = _rms_rows(x1, n2_ref[...]).astype(BF16)


def _merge(x, a_out, b_out, c_out, gates, w_pa, w_pb, w_pc, w_o, norm2, *, tm):
    n, d_model = x.shape
    row = lambda arr: pl.BlockSpec((tm, arr.shape[1]), lambda i: (i, 0))
    full = lambda arr: pl.BlockSpec(arr.shape, lambda i: (0,) * arr.ndim)
    rows = (x, a_out, b_out, c_out, gates)
    consts = (w_pa, w_pb, w_pc, w_o, norm2)
    return pl.pallas_call(
        functools.partial(_merge_body, d_model=d_model),
        grid=(n // tm,),
        in_specs=[row(r) for r in rows] + [full(c) for c in consts],
        out_specs=[row(x), row(x)],
        out_shape=(jax.ShapeDtypeStruct((n, d_model), F32), jax.ShapeDtypeStruct((n, d_model), BF16)),
        compiler_params=_cparams(1),
        name="merge",
    )(*rows, *consts)


def _swiglu_body(x_ref, h_ref, wg_ref, wu_ref, wd_ref, o_ref, *, ff_chunk):
    h = h_ref[...]
    d_ff = wg_ref.shape[1]
    acc = x_ref[...]
    for c0 in range(0, d_ff, ff_chunk):
        gate = _dot(h, wg_ref[:, c0:c0 + ff_chunk])
        up = _dot(h, wu_ref[:, c0:c0 + ff_chunk])
        act = (gate * _sigmoid(gate) * up).astype(BF16)
        acc = acc + _dot(act, wd_ref[c0:c0 + ff_chunk, :])
    o_ref[...] = acc


def _swiglu(x, h, w_gate, w_up, w_down, *, tm, ff_chunk):
    n, d_model = x.shape
    row = lambda arr: pl.BlockSpec((tm, arr.shape[1]), lambda i: (i, 0))
    full = lambda arr: pl.BlockSpec(arr.shape, lambda i: (0,) * arr.ndim)
    return pl.pallas_call(
        functools.partial(_swiglu_body, ff_chunk=ff_chunk),
        grid=(n // tm,),
        in_specs=[row(x), row(h), full(w_gate), full(w_up), full(w_down)],
        out_specs=row(x),
        out_shape=jax.ShapeDtypeStruct((n, d_model), F32),
        compiler_params=_cparams(1),
        name="swiglu",
    )(x, h, w_gate, w_up, w_down)


def _row_tile(n, target):
    tm = min(n, target)
    assert n % tm == 0
    return tm


def _ff_chunk(d_ff):
    for c in (512, 256, 128):
        if d_ff % c == 0:
            return c
    return d_ff


def kernel(x_prompt, x_sample, cache_k, cache_v, state_pool, page_table, norm1, w_in, w_pool, pool_scale,
           w_s, b_s, q_gain, k_gain, sb_bias, w_pa, w_pb, w_pc, w_o, norm2, w_gate, w_up, w_down):
    depth = w_in.shape[0]
    bsz, seq, d_model = x_prompt.shape
    nseq, dec_seq, _ = x_sample.shape
    n_pool, page_size, n_heads, head_dim = cache_k.shape[1:]
    n_pages = page_table.shape[1]
    past_len = n_pages * page_size
    sb_dim = n_heads * head_dim
    d_ff = w_gate.shape[2]
    assert head_dim == HEAD_DIM and w_pool.shape[1] == N_GROUPS and w_s.shape[2] == CHUNK
    assert seq % CHUNK == 0 and past_len % CHUNK == 0 and dec_seq <= CHUNK

    lane = jnp.arange(sb_dim)
    hmean = jnp.where((lane[:, None] // HEAD_DIM) == (lane[None, :] // HEAD_DIM), 1.0 / HEAD_DIM, 0.0).astype(BF16)
    kidx = jnp.arange(LANES)
    suffix_ones = jnp.concatenate(
        [(kidx[:, None] > kidx[None, :]).astype(BF16), jnp.ones((LANES, LANES), BF16)], axis=1)
    w_in_b = w_in.astype(BF16)
    w_pa_b, w_pb_b, w_pc_b, w_o_b = (w.astype(BF16) for w in (w_pa, w_pb, w_pc, w_o))
    w_gate_b, w_up_b, w_down_b = (w.astype(BF16) for w in (w_gate, w_up, w_down))
    w_s_b = w_s.astype(BF16)
    grp = jnp.arange(MIX_DIM) // GROUP_DIM
    wpool_bd = jnp.where(grp[None, :, None] == grp[None, None, :],
                         jnp.tile(w_pool.reshape(depth, MIX_DIM, GROUP_DIM), (1, 1, N_GROUPS)), 0.0).astype(BF16)
    bsf = jnp.repeat(jnp.swapaxes(b_s, 1, 2), GROUP_DIM, axis=2)
    wrow = jnp.repeat(jnp.transpose(w_s[:, :, :dec_seq, :dec_seq], (0, 2, 3, 1)), GROUP_DIM, axis=3)
    qg = jnp.tile(q_gain, (1, n_heads)) * (HEAD_DIM ** -0.5 * LOG2_E)
    bias2 = sb_bias * LOG2_E
    kg = jnp.tile(k_gain, (1, n_heads))
    cache_kt = jnp.transpose(cache_k, (0, 1, 3, 4, 2)).reshape(depth, n_pool, sb_dim, page_size)
    cache_vt = jnp.transpose(cache_v, (0, 1, 3, 4, 2)).reshape(depth, n_pool, sb_dim, page_size)

    n_p = bsz * seq
    n_s = nseq * dec_seq
    tm_p = _row_tile(n_p, 512)
    tm_s = _row_tile(n_s, 512)
    tq = _row_tile(seq, 512)
    ff_chunk = _ff_chunk(d_ff)

    xp = x_prompt.reshape(n_p, d_model)
    xs = jnp.swapaxes(x_sample, 0, 1).reshape(n_s, d_model)
    outs = [[] for _ in range(7)]
    for l in range(depth):
        lw = dict(norm1=norm1[l][None], w_in=w_in_b[l], qg=qg[l][None], kg=kg[l][None])
        a, u, vb, q, k32, kb, v32, vbf, gates = _in_proj(
            xp, lw["norm1"], lw["w_in"], lw["qg"], lw["kg"], hmean, tm=tm_p)
        a3 = a.reshape(bsz, seq, MIX_DIM)
        a_out, b_out = _mix_prompt(a3, u.reshape(bsz, seq, MIX_DIM), vb.reshape(bsz, seq, MIX_DIM),
                                   w_s_b[l], bsf[l], wpool_bd[l], pool_scale[l][None])
        c_out = _attn_prompt(q.reshape(bsz, seq, sb_dim), kb.reshape(bsz, seq, sb_dim),
                             vbf.reshape(bsz, seq, sb_dim), bias2[l], suffix_ones, tq=tq)
        x1, h2 = _merge(xp, a_out.reshape(n_p, MIX_DIM), b_out.reshape(n_p, MIX_DIM), c_out.reshape(n_p, sb_dim),
                        gates, w_pa_b[l], w_pb_b[l], w_pc_b[l], w_o_b[l], norm2[l][None], tm=tm_p)
        xp = _swiglu(x1, h2, w_gate_b[l], w_up_b[l], w_down_b[l], tm=tm_p, ff_chunk=ff_chunk)
        outs[0].append(k32.reshape(bsz, seq, n_heads, head_dim))
        outs[1].append(v32.reshape(bsz, seq, n_heads, head_dim))
        outs[2].append(a3[:, seq - POOL_BUF:, :])
        a, u, vb, q, k32, kb, v32, vbf, gates = _in_proj(
            xs, lw["norm1"], lw["w_in"], lw["qg"], lw["kg"], hmean, tm=tm_s)
        tmaj = lambda arr: arr.reshape(dec_seq, nseq, arr.shape[-1])
        smaj = lambda arr: jnp.swapaxes(tmaj(arr), 0, 1)
        a_out, b_out, new_pool = _mix_sample(
            tmaj(a), jnp.swapaxes(state_pool[l], 0, 1), tmaj(u), tmaj(vb),
            wrow[l], bsf[l], wpool_bd[l], pool_scale[l][None], first_pos=past_len)
        k_new = smaj(k32)
        v_new = smaj(v32)
        c_out = _attn_sample(page_table, bias2[l], smaj(q.astype(F32)), k_new, v_new,
                             cache_kt, cache_vt, suffix_ones, layer=l)
        c_out = jnp.swapaxes(c_out, 0, 1).reshape(n_s, sb_dim).astype(BF16)
        x1, h2 = _merge(xs, a_out.reshape(n_s, MIX_DIM), b_out.reshape(n_s, MIX_DIM), c_out,
                        gates, w_pa_b[l], w_pb_b[l], w_pc_b[l], w_o_b[l], norm2[l][None], tm=tm_s)
        xs = _swiglu(x1, h2, w_gate_b[l], w_up_b[l], w_down_b[l], tm=tm_s, ff_chunk=ff_chunk)
        outs[3].append(k_new.reshape(nseq, dec_seq, n_heads, head_dim))
        outs[4].append(v_new.reshape(nseq, dec_seq, n_heads, head_dim))
        outs[5].append(jnp.swapaxes(new_pool, 0, 1))
        outs[6].append(smaj(vb))
    y_prompt = xp.reshape(bsz, seq, d_model)
    y_sample = jnp.swapaxes(xs.reshape(dec_seq, nseq, d_model), 0, 1)
    return (y_prompt, y_sample) + tuple(jnp.stack(o) for o in outs)
```

```python
import functools

import jax
import jax.numpy as jnp
from jax import lax
from jax.experimental import pallas as pl
from jax.experimental.pallas import tpu as pltpu

F32 = jnp.float32
BF16 = jnp.bfloat16

NORM_EPS = 1e-6
POOL_WINDOWS = (2, 4, 8, 16)
POOL_BUF = 15
GROUP_DIM = 64
N_GROUPS = 4
MIX_DIM = N_GROUPS * GROUP_DIM
CHUNK = 128
HEAD_DIM = 64
LANES = 128
LOG2_E = 1.4426950408889634
PROMPT_KEY_BLOCK = 256
UNROLL_KEY_BLOCKS = 4
VMEM_LIMIT_BYTES = 56 * 1024 * 1024


def _cparams(n_grid_dims):
    return pltpu.CompilerParams(
        dimension_semantics=("arbitrary",) * n_grid_dims,
        vmem_limit_bytes=VMEM_LIMIT_BYTES)


def _dot(a, b):
    return jnp.dot(a, b, preferred_element_type=F32)


def _dot_nt(a, b):
    return lax.dot_general(a, b, (((1,), (1,)), ((), ())), preferred_element_type=F32)


def _rms_rows(x, gain):
    ms = jnp.mean(x * x, axis=-1, keepdims=True)
    return (x * lax.rsqrt(ms + NORM_EPS)) * gain


def _sigmoid(x):
    return 1.0 / (1.0 + jnp.exp(-x))


def _in_proj_body(x_ref, n1_ref, w_ref, qg_ref, kg_ref, hmean_ref,
                  a_ref, u_ref, vb_ref, q_ref, k_ref, v_ref, g_ref,
                  *, sb_dim, d_model, transpose_kv):
    hb = _rms_rows(x_ref[...], n1_ref[...]).astype(BF16)

    def proj(c0, width):
        return _dot(hb, w_ref[:, c0:c0 + width])

    def head_rms(y, gain):
        ms = _dot((y * y).astype(BF16), hmean_ref[...])
        return (y * lax.rsqrt(ms + NORM_EPS)) * gain

    def store_kv(ref, y):
        if transpose_kv:
            ref[...] = jnp.transpose(y).reshape(ref.shape)
        else:
            ref[...] = y

    c = 0
    a_ref[...] = proj(c, MIX_DIM)
    c += MIX_DIM
    u_ref[...] = proj(c, MIX_DIM)
    c += MIX_DIM
    vb_ref[...] = proj(c, MIX_DIM)
    c += MIX_DIM
    q_ref[...] = head_rms(proj(c, sb_dim), qg_ref[...]).astype(BF16)
    c += sb_dim
    store_kv(k_ref, head_rms(proj(c, sb_dim), kg_ref[...]))
    c += sb_dim
    store_kv(v_ref, proj(c, sb_dim))
    c += sb_dim
    for j in range(3):
        g_ref[:, j * d_model:(j + 1) * d_model] = _sigmoid(proj(c, d_model)).astype(BF16)
        c += d_model


def _in_proj(x, norm1, w_in, q_gain_s, k_gain_t, hmean, *, tm, seq_len=None):
    n, d_model = x.shape
    sb_dim = hmean.shape[0]
    row = lambda width: pl.BlockSpec((tm, width), lambda i: (i, 0))
    full = lambda arr: pl.BlockSpec(arr.shape, lambda i: (0,) * arr.ndim)
    if seq_len is None:
        kv_shape = jax.ShapeDtypeStruct((n, sb_dim), F32)
        kv_spec = row(sb_dim)
    else:
        tiles = seq_len // tm
        n_heads = sb_dim // HEAD_DIM
        kv_shape = jax.ShapeDtypeStruct((n // seq_len, n_heads, HEAD_DIM, seq_len), F32)
        kv_spec = pl.BlockSpec((None, n_heads, HEAD_DIM, tm), lambda i: (i // tiles, 0, 0, i % tiles))
    out_shapes = (
        jax.ShapeDtypeStruct((n, MIX_DIM), F32),
        jax.ShapeDtypeStruct((n, MIX_DIM), F32),
        jax.ShapeDtypeStruct((n, MIX_DIM), F32),
        jax.ShapeDtypeStruct((n, sb_dim), BF16),
        kv_shape,
        kv_shape,
        jax.ShapeDtypeStruct((n, 3 * d_model), BF16),
    )
    out_specs = [row(MIX_DIM)] * 3 + [row(sb_dim), kv_spec, kv_spec, row(3 * d_model)]
    return pl.pallas_call(
        functools.partial(_in_proj_body, sb_dim=sb_dim, d_model=d_model, transpose_kv=seq_len is not None),
        grid=(n // tm,),
        in_specs=[row(d_model), full(norm1), full(w_in), full(q_gain_s), full(k_gain_t), full(hmean)],
        out_specs=out_specs,
        out_shape=out_shapes,
        compiler_params=_cparams(1),
        name="in_proj",
    )(x, norm1, w_in, q_gain_s, k_gain_t, hmean)


def _group_of_lane(shape):
    return lax.broadcasted_iota(jnp.int32, shape, len(shape) - 1) // GROUP_DIM


def _mix_prompt_body(a_ref, halo_ref, u_ref, vb_ref, ws_ref, bsf_ref, wpool_ref, pscale_ref,
                     ao_ref, bo_ref, e_ref, s2_ref, s4_ref, s8_ref, *, rows):
    c = pl.program_id(1)
    end = 32 + rows
    a = a_ref[...]
    e_ref[0:16, :] = jnp.zeros((16, MIX_DIM), F32)
    e_ref[16:32, :] = jnp.where(c > 0, halo_ref[...], 0.0)
    e_ref[32:end, :] = a
    s2_ref[8:end, :] = e_ref[8:end, :] + e_ref[7:end - 1, :]
    s4_ref[16:end, :] = s2_ref[16:end, :] + s2_ref[14:end - 2, :]
    s8_ref[24:end, :] = s4_ref[24:end, :] + s4_ref[20:end - 4, :]
    s16 = s8_ref[32:end, :] + s8_ref[24:end - 8, :]
    grp = _group_of_lane((rows, MIX_DIM))
    row = lax.broadcasted_iota(jnp.int32, (rows, MIX_DIM), 0)
    s = jnp.where(grp == 0, s2_ref[32:end, :],
                  jnp.where(grp == 1, s4_ref[32:end, :],
                            jnp.where(grp == 2, s8_ref[32:end, :], s16)))
    window = jnp.left_shift(2, grp)
    cnt = jnp.minimum(c * rows + row + 1, window).astype(F32)
    d = s / cnt - a
    ao_ref[...] = (_dot(d.astype(BF16), wpool_ref[...]) * pscale_ref[...]).astype(BF16)
    r128 = lax.broadcasted_iota(jnp.int32, (CHUNK, CHUNK), 0)
    c128 = lax.broadcasted_iota(jnp.int32, (CHUNK, CHUNK), 1)
    wg = [jnp.where(c128 <= r128, ws_ref[g], jnp.zeros((), BF16)) for g in range(N_GROUPS)]
    grp_c = _group_of_lane((CHUNK, MIX_DIM))
    for r0 in range(0, rows, CHUNK):
        vbb = vb_ref[r0:r0 + CHUNK, :].astype(BF16)
        o = jnp.zeros((CHUNK, MIX_DIM), F32)
        for g in range(N_GROUPS):
            o = jnp.where(grp_c == g, _dot(wg[g], vbb), o)
        bo_ref[r0:r0 + CHUNK, :] = (u_ref[r0:r0 + CHUNK, :] * (o + bsf_ref[...])).astype(BF16)


def _mix_prompt(a, u, vb, ws, bsf, wpool_bd, pscale, *, rows):
    bsz, t, _ = a.shape
    assert rows % CHUNK == 0 and t % rows == 0
    blk = pl.BlockSpec((None, rows, MIX_DIM), lambda b, c: (b, c, 0))
    halo = pl.BlockSpec((None, 16, MIX_DIM), lambda b, c: (b, jnp.maximum(c * (rows // 16) - 1, 0), 0))
    full = lambda arr: pl.BlockSpec(arr.shape, lambda b, c: (0,) * arr.ndim)
    out = jax.ShapeDtypeStruct((bsz, t, MIX_DIM), BF16)
    return pl.pallas_call(
        functools.partial(_mix_prompt_body, rows=rows),
        grid=(bsz, t // rows),
        in_specs=[blk, halo, blk, blk, full(ws), full(bsf), full(wpool_bd), full(pscale)],
        out_specs=[blk, blk],
        out_shape=(out, out),
        scratch_shapes=[pltpu.VMEM((32 + rows, MIX_DIM), F32)] * 4,
        compiler_params=_cparams(2),
        name="mix_prompt",
    )(a, a, u, vb, ws, bsf, wpool_bd, pscale)


def _mix_sample_body(a_ref, prev_ref, u_ref, vb_ref, wrow_ref, bsf_ref, wpool_ref, pscale_ref,
                     ao_ref, bo_ref, np_ref, *, dec_seq, first_pos):
    nseq = a_ref.shape[1]
    grp = _group_of_lane((nseq, MIX_DIM))
    e = [prev_ref[j] for j in range(POOL_BUF)] + [a_ref[t] for t in range(dec_seq)]
    n_e = len(e)
    sums = []
    prev_level, span = e, 1
    for _ in POOL_WINDOWS:
        level = [None] * n_e
        for j in range(n_e):
            if j - span >= 0 and prev_level[j] is not None and prev_level[j - span] is not None:
                level[j] = prev_level[j] + prev_level[j - span]
        sums.append(level)
        prev_level, span = level, span * 2
    for t in range(dec_seq):
        j = POOL_BUF + t
        mean = None
        for k, w in enumerate(POOL_WINDOWS):
            m = sums[k][j] / float(min(first_pos + t + 1, w))
            mean = m if mean is None else jnp.where(grp == k, m, mean)
        d = mean - e[j]
        ao_ref[t] = (_dot(d.astype(BF16), wpool_ref[...]) * pscale_ref[...]).astype(BF16)
        o = bsf_ref[t:t + 1, :]
        for s in range(t + 1):
            o = o + wrow_ref[t, s:s + 1, :] * vb_ref[s]
        bo_ref[t] = (u_ref[t] * o).astype(BF16)
    for j in range(POOL_BUF):
        np_ref[j] = e[n_e - POOL_BUF + j]


def _mix_sample(a, prev, u, vb, wrow, bsf, wpool_bd, pscale, *, first_pos):
    dec_seq, nseq, _ = a.shape
    full = lambda arr: pl.BlockSpec(arr.shape, lambda i: (0,) * arr.ndim)
    args = (a, prev, u, vb, wrow, bsf, wpool_bd, pscale)
    out_shapes = (jax.ShapeDtypeStruct((dec_seq, nseq, MIX_DIM), BF16),
                  jax.ShapeDtypeStruct((dec_seq, nseq, MIX_DIM), BF16),
                  jax.ShapeDtypeStruct((POOL_BUF, nseq, MIX_DIM), F32))
    return pl.pallas_call(
        functools.partial(_mix_sample_body, dec_seq=dec_seq, first_pos=first_pos),
        grid=(1,),
        in_specs=[full(x) for x in args],
        out_specs=[full(s) for s in out_shapes],
        out_shape=out_shapes,
        compiler_params=_cparams(1),
        name="mix_sample",
    )(*args)


def _sb_logs(z):
    neg_part = jnp.minimum(z, 0.0)
    neg_rest = neg_part - z
    softplus = jnp.log2(1.0 + jnp.exp2(neg_part + neg_rest))
    return neg_part - softplus, neg_rest - softplus


def _attn_prompt_body(bias_ref, q_ref, k_ref, v_ref, su_ref, o_ref, carry_ref, acc_ref, qa_ref, *, tq, kb):
    hp = pl.program_id(1)
    i = pl.program_id(2)
    n_diag = tq // kb
    su = su_ref[...]
    zeros = jnp.zeros((HEAD_DIM, kb), BF16)
    carry_ref[...] = jnp.zeros(carry_ref.shape, F32)
    acc_ref[...] = jnp.zeros(acc_ref.shape, F32)
    lane = lax.broadcasted_iota(jnp.int32, (1, 2 * kb), 1)
    bias_row = jnp.where(lane < kb, bias_ref[hp * 2], bias_ref[hp * 2 + 1])
    b_hi = bias_row.astype(BF16).astype(F32)
    b_mid = (bias_row - b_hi).astype(BF16).astype(F32)
    b_lo = (bias_row - b_hi) - b_mid
    chan = lax.broadcasted_iota(jnp.int32, (LANES, 2 * kb), 0)
    bias_rows = jnp.where(chan == 0, b_hi, jnp.where(chan == 1, b_mid, jnp.where(chan == 2, b_lo, 0.0))).astype(BF16)
    qlane = lax.broadcasted_iota(jnp.int32, (tq, LANES), 1)
    qa_ref[:, 0:LANES] = q_ref[...]
    qa_ref[:, LANES:2 * LANES] = jnp.where(qlane < 3, 1.0, 0.0).astype(BF16)

    def block_diag(ref, off):
        t = ref[:, :, pl.ds(off, kb)].astype(BF16)
        return jnp.concatenate([jnp.concatenate([t[0], zeros], axis=1),
                                jnp.concatenate([zeros, t[1]], axis=1)], axis=0)

    def step(j, row0, masked):
        rows = tq - row0
        off = pl.multiple_of(j * kb, kb)
        z = _dot(qa_ref[row0:tq, :], jnp.concatenate([block_diag(k_ref, off), bias_rows], axis=0))
        log_beta, log_rem = _sb_logs(z)
        if masked:
            kpos = j * kb + lax.broadcasted_iota(jnp.int32, (rows, 2 * kb), 1) % kb
            qpos = i * tq + row0 + lax.broadcasted_iota(jnp.int32, (rows, 2 * kb), 0)
            mask = kpos < qpos
            log_rem = jnp.where(mask, log_rem, 0.0)
        carries, suffix = [], []
        for hh in range(2):
            lr = log_rem[:, hh * kb:(hh + 1) * kb]
            suffix.append(_dot(lr.astype(BF16), su))
            carry = carry_ref[hh, row0:tq, :]
            carries.append(jnp.concatenate([carry] * (kb // LANES), axis=1))
            carry_ref[hh, row0:tq, :] = carry + jnp.sum(lr, axis=-1, keepdims=True)
        att = jnp.exp2(log_beta + jnp.concatenate(suffix, axis=1) + jnp.concatenate(carries, axis=1))
        if masked:
            att = jnp.where(mask, att, 0.0)
        acc_ref[row0:tq, :] += _dot_nt(att.astype(BF16), block_diag(v_ref, off))

    first = i * n_diag
    for dj in reversed(range(n_diag)):
        step(first + dj, dj * kb, True)

    def run(top, n_blocks, unroll):
        def body(t, c):
            for dj in range(unroll):
                step(top - 1 - dj - unroll * t, 0, False)
            return c
        lax.fori_loop(0, n_blocks // unroll, body, 0)

    n_main = (first // UNROLL_KEY_BLOCKS) * UNROLL_KEY_BLOCKS
    run(first, n_main, UNROLL_KEY_BLOCKS)
    if UNROLL_KEY_BLOCKS != n_diag:
        assert UNROLL_KEY_BLOCKS % n_diag == 0
        run(first - n_main, first - n_main, n_diag)
    o_ref[...] = acc_ref[...].astype(o_ref.dtype)


def _attn_prompt(q, kt, vt, bias, suffix_upper, *, tq):
    bsz, t, sb_dim = q.shape
    n_pairs = sb_dim // LANES
    kb = suffix_upper.shape[0]
    qspec = pl.BlockSpec((None, tq, LANES), lambda b, hp, i: (b, i, hp))
    kspec = pl.BlockSpec((None, 2, HEAD_DIM, t), lambda b, hp, i: (b, hp, 0, 0))
    return pl.pallas_call(
        functools.partial(_attn_prompt_body, tq=tq, kb=kb),
        grid=(bsz, n_pairs, t // tq),
        in_specs=[pl.BlockSpec(memory_space=pltpu.SMEM), qspec, kspec, kspec,
                  pl.BlockSpec(suffix_upper.shape, lambda b, hp, i: (0, 0))],
        out_specs=qspec,
        out_shape=jax.ShapeDtypeStruct((bsz, t, sb_dim), BF16),
        scratch_shapes=[pltpu.VMEM((2, tq, LANES), F32), pltpu.VMEM((tq, LANES), F32),
                        pltpu.VMEM((tq, 2 * LANES), BF16)],
        compiler_params=_cparams(3),
        name="attn_prompt",
    )(bias, q, kt, vt, suffix_upper)


def _attn_sample_body(pt_ref, bias_ref, q_ref, kn_ref, vn_ref, *rest, n_heads, dec_seq, n_pages):
    del pt_ref
    kpage_refs = rest[:n_pages]
    vpage_refs = rest[n_pages:2 * n_pages]
    so_ref, o_ref = rest[2 * n_pages:]
    rows = n_heads * dec_seq
    sb_dim = n_heads * HEAD_DIM
    lane = lax.broadcasted_iota(jnp.int32, (dec_seq, sb_dim), 1)
    in_head = [(lane >= h * HEAD_DIM) & (lane < (h + 1) * HEAD_DIM) for h in range(n_heads)]
    q = q_ref[...]
    qbd = jnp.concatenate([jnp.where(in_head[h], q, 0.0) for h in range(n_heads)], axis=0).astype(BF16)
    biasb = jnp.concatenate([jnp.full((dec_seq, LANES), bias_ref[h], F32) for h in range(n_heads)], axis=0)
    pad = jnp.zeros((LANES - dec_seq, sb_dim), F32)
    kn = jnp.concatenate([kn_ref[...], pad], axis=0).astype(BF16)
    vn = jnp.concatenate([vn_ref[...], pad], axis=0).astype(BF16)
    kidx = lax.broadcasted_iota(jnp.int32, (rows, LANES), 1)
    trow = lax.broadcasted_iota(jnp.int32, (rows, LANES), 0) % dec_seq
    new_mask = kidx < trow
    pages = list(reversed(range(n_pages)))
    z = jnp.concatenate([_dot_nt(qbd, kn) + biasb]
                        + [_dot(qbd, kpage_refs[pg][...].astype(BF16)) + biasb for pg in pages], axis=0)
    log_beta, log_rem = _sb_logs(z)
    log_rem = jnp.concatenate([jnp.where(new_mask, log_rem[:rows], 0.0), log_rem[rows:]], axis=0)
    cs = _dot(log_rem.astype(BF16), so_ref[...])
    carries = [jnp.zeros((rows, LANES), F32)]
    for p in range(n_pages):
        carries.append(carries[-1] + cs[p * rows:(p + 1) * rows, LANES:])
    att = jnp.exp2(log_beta + cs[:, :LANES] + jnp.concatenate(carries, axis=0))
    acc = _dot(jnp.where(new_mask, att[:rows], 0.0).astype(BF16), vn)
    for p, pg in enumerate(pages):
        acc = acc + _dot_nt(att[(p + 1) * rows:(p + 2) * rows].astype(BF16), vpage_refs[pg][...].astype(BF16))
    out = jnp.zeros((dec_seq, sb_dim), F32)
    for h in range(n_heads):
        out = jnp.where(in_head[h], acc[h * dec_seq:(h + 1) * dec_seq, :], out)
    o_ref[...] = out


def _attn_sample(page_table, bias, q, k_new, v_new, cache_kt, cache_vt, suffix_ones, *, layer):
    nseq, dec_seq, sb_dim = q.shape
    n_heads = sb_dim // HEAD_DIM
    n_pages = page_table.shape[1]
    page_size = cache_kt.shape[-1]
    assert page_size == LANES
    seq_spec = pl.BlockSpec((None, dec_seq, sb_dim), lambda n, pt: (n, 0, 0))

    def page_spec(pg):
        return pl.BlockSpec((None, None, sb_dim, page_size), lambda n, pt: (layer, pt[n, pg], 0, 0))

    page_specs = [page_spec(pg) for pg in range(n_pages)]
    grid_spec = pltpu.PrefetchScalarGridSpec(
        num_scalar_prefetch=1,
        grid=(nseq,),
        in_specs=[pl.BlockSpec(memory_space=pltpu.SMEM), seq_spec, seq_spec, seq_spec]
        + page_specs + page_specs
        + [pl.BlockSpec(suffix_ones.shape, lambda n, pt: (0, 0))],
        out_specs=seq_spec,
    )
    return pl.pallas_call(
        functools.partial(_attn_sample_body, n_heads=n_heads, dec_seq=dec_seq, n_pages=n_pages),
        grid_spec=grid_spec,
        out_shape=jax.ShapeDtypeStruct((nseq, dec_seq, sb_dim), F32),
        compiler_params=_cparams(1),
        name="attn_sample",
    )(page_table, bias, q, k_new, v_new, *([cache_kt] * n_pages), *([cache_vt] * n_pages), suffix_ones)


def _merge_body(x_ref, a_ref, b_ref, c_ref, g_ref, wpa_ref, wpb_ref, wpc_ref, wo_ref, n2_ref,
                x1_ref, h2_ref, *, d_model):
    merged = (g_ref[:, 0:d_model].astype(F32) * _dot(a_ref[...], wpa_ref[...])
              + g_ref[:, d_model:2 * d_model].astype(F32) * _dot(b_ref[...], wpb_ref[...])
              + g_ref[:, 2 * d_model:3 * d_model].astype(F32) * _dot(c_ref[...], wpc_ref[...]))
    x1 = x_ref[...] + _dot(merged.astype(BF16), wo_ref[...])
    x1_ref[...] = x1
    h2_ref[...] = _rms_rows(x1, n2_ref[...]).astype(BF16)


def _merge(x, a_out, b_out, c_out, gates, w_pa, w_pb, w_pc, w_o, norm2, *, tm):
    n, d_model = x.shape
    row = lambda arr: pl.BlockSpec((tm, arr.shape[1]), lambda i: (i, 0))
    full = lambda arr: pl.BlockSpec(arr.shape, lambda i: (0,) * arr.ndim)
    rows = (x, a_out, b_out, c_out, gates)
    consts = (w_pa, w_pb, w_pc, w_o, norm2)
    return pl.pallas_call(
        functools.partial(_merge_body, d_model=d_model),
        grid=(n // tm,),
        in_specs=[row(r) for r in rows] + [full(c) for c in consts],
        out_specs=[row(x), row(x)],
        out_shape=(jax.ShapeDtypeStruct((n, d_model), F32), jax.ShapeDtypeStruct((n, d_model), BF16)),
        compiler_params=_cparams(1),
        name="merge",
    )(*rows, *consts)


def _swiglu_body(x_ref, h_ref, wg_ref, wu_ref, wd_ref, o_ref, *, ff_chunk):
    h = h_ref[...]
    d_ff = wg_ref.shape[1]
    acc = x_ref[...]
    for c0 in range(0, d_ff, ff_chunk):
        gate = _dot(h, wg_ref[:, c0:c0 + ff_chunk])
        up = _dot(h, wu_ref[:, c0:c0 + ff_chunk])
        act = (gate * _sigmoid(gate) * up).astype(BF16)
        acc = acc + _dot(act, wd_ref[c0:c0 + ff_chunk, :])
    o_ref[...] = acc


def _swiglu(x, h, w_gate, w_up, w_down, *, tm, ff_chunk):
    n, d_model = x.shape
    row = lambda arr: pl.BlockSpec((tm, arr.shape[1]), lambda i: (i, 0))
    full = lambda arr: pl.BlockSpec(arr.shape, lambda i: (0,) * arr.ndim)
    return pl.pallas_call(
        functools.partial(_swiglu_body, ff_chunk=ff_chunk),
        grid=(n // tm,),
        in_specs=[row(x), row(h), full(w_gate), full(w_up), full(w_down)],
        out_specs=row(x),
        out_shape=jax.ShapeDtypeStruct((n, d_model), F32),
        compiler_params=_cparams(1),
        name="swiglu",
    )(x, h, w_gate, w_up, w_down)


def _row_tile(n, target):
    tm = min(n, target)
    assert n % tm == 0
    return tm


def _ff_chunk(d_ff):
    for c in (512, 256, 128):
        if d_ff % c == 0:
            return c
    return d_ff


def kernel(x_prompt, x_sample, cache_k, cache_v, state_pool, page_table, norm1, w_in, w_pool, pool_scale,
           w_s, b_s, q_gain, k_gain, sb_bias, w_pa, w_pb, w_pc, w_o, norm2, w_gate, w_up, w_down):
    depth = w_in.shape[0]
    bsz, seq, d_model = x_prompt.shape
    nseq, dec_seq, _ = x_sample.shape
    n_pool, page_size, n_heads, head_dim = cache_k.shape[1:]
    n_pages = page_table.shape[1]
    past_len = n_pages * page_size
    sb_dim = n_heads * head_dim
    d_ff = w_gate.shape[2]
    assert head_dim == HEAD_DIM and w_pool.shape[1] == N_GROUPS and w_s.shape[2] == CHUNK
    assert seq % CHUNK == 0 and past_len % CHUNK == 0 and dec_seq <= CHUNK

    lane = jnp.arange(sb_dim)
    hmean = jnp.where((lane[:, None] // HEAD_DIM) == (lane[None, :] // HEAD_DIM), 1.0 / HEAD_DIM, 0.0).astype(BF16)
    kidx = jnp.arange(LANES)
    suffix_ones = jnp.concatenate(
        [(kidx[:, None] > kidx[None, :]).astype(BF16), jnp.ones((LANES, LANES), BF16)], axis=1)
    kb_p = min(PROMPT_KEY_BLOCK, seq)
    kidx_p = jnp.arange(kb_p)
    suffix_upper = (kidx_p[:, None] > kidx_p[None, :]).astype(BF16)
    w_in_b = w_in.astype(BF16)
    w_pa_b, w_pb_b, w_pc_b, w_o_b = (w.astype(BF16) for w in (w_pa, w_pb, w_pc, w_o))
    w_gate_b, w_up_b, w_down_b = (w.astype(BF16) for w in (w_gate, w_up, w_down))
    w_s_b = w_s.astype(BF16)
    grp = jnp.arange(MIX_DIM) // GROUP_DIM
    wpool_bd = jnp.where(grp[None, :, None] == grp[None, None, :],
                         jnp.tile(w_pool.reshape(depth, MIX_DIM, GROUP_DIM), (1, 1, N_GROUPS)), 0.0).astype(BF16)
    bsf = jnp.repeat(jnp.swapaxes(b_s, 1, 2), GROUP_DIM, axis=2)
    wrow = jnp.repeat(jnp.transpose(w_s[:, :, :dec_seq, :dec_seq], (0, 2, 3, 1)), GROUP_DIM, axis=3)
    qg = jnp.tile(q_gain, (1, n_heads)) * (HEAD_DIM ** -0.5 * LOG2_E)
    bias2 = sb_bias * LOG2_E
    kg = jnp.tile(k_gain, (1, n_heads))
    cache_kt = jnp.transpose(cache_k, (0, 1, 3, 4, 2)).reshape(depth, n_pool, sb_dim, page_size)
    cache_vt = jnp.transpose(cache_v, (0, 1, 3, 4, 2)).reshape(depth, n_pool, sb_dim, page_size)

    n_p = bsz * seq
    n_s = nseq * dec_seq
    tm_p = _row_tile(n_p, 512)
    tm_s = _row_tile(n_s, 512)
    tq = _row_tile(seq, 512)
    ff_chunk = _ff_chunk(d_ff)

    xp = x_prompt.reshape(n_p, d_model)
    xs = jnp.swapaxes(x_sample, 0, 1).reshape(n_s, d_model)
    outs = [[] for _ in range(7)]
    for l in range(depth):
        lw = dict(norm1=norm1[l][None], w_in=w_in_b[l], qg=qg[l][None], kg=kg[l][None])
        a, u, vb, q, kt, vt, gates = _in_proj(
            xp, lw["norm1"], lw["w_in"], lw["qg"], lw["kg"], hmean, tm=tm_p, seq_len=seq)
        a3 = a.reshape(bsz, seq, MIX_DIM)
        a_out, b_out = _mix_prompt(a3, u.reshape(bsz, seq, MIX_DIM), vb.reshape(bsz, seq, MIX_DIM),
                                   w_s_b[l], bsf[l], wpool_bd[l], pool_scale[l][None], rows=tq)
        c_out = _attn_prompt(q.reshape(bsz, seq, sb_dim), kt, vt, bias2[l], suffix_upper, tq=tq)
        x1, h2 = _merge(xp, a_out.reshape(n_p, MIX_DIM), b_out.reshape(n_p, MIX_DIM), c_out.reshape(n_p, sb_dim),
                        gates, w_pa_b[l], w_pb_b[l], w_pc_b[l], w_o_b[l], norm2[l][None], tm=tm_p)
        xp = _swiglu(x1, h2, w_gate_b[l], w_up_b[l], w_down_b[l], tm=tm_p, ff_chunk=ff_chunk)
        outs[0].append(kt)
        outs[1].append(vt)
        outs[2].append(a3[:, seq - POOL_BUF:, :])
        a, u, vb, q, k32, v32, gates = _in_proj(
            xs, lw["norm1"], lw["w_in"], lw["qg"], lw["kg"], hmean, tm=tm_s)
        tmaj = lambda arr: arr.reshape(dec_seq, nseq, arr.shape[-1])
        smaj = lambda arr: jnp.swapaxes(tmaj(arr), 0, 1)
        a_out, b_out, new_pool = _mix_sample(
            tmaj(a), jnp.swapaxes(state_pool[l], 0, 1), tmaj(u), tmaj(vb),
            wrow[l], bsf[l], wpool_bd[l], pool_scale[l][None], first_pos=past_len)
        k_new = smaj(k32)
        v_new = smaj(v32)
        c_out = _attn_sample(page_table, bias2[l], smaj(q.astype(F32)), k_new, v_new,
                             cache_kt, cache_vt, suffix_ones, layer=l)
        c_out = jnp.swapaxes(c_out, 0, 1).reshape(n_s, sb_dim).astype(BF16)
        x1, h2 = _merge(xs, a_out.reshape(n_s, MIX_DIM), b_out.reshape(n_s, MIX_DIM), c_out,
                        gates, w_pa_b[l], w_pb_b[l], w_pc_b[l], w_o_b[l], norm2[l][None], tm=tm_s)
        xs = _swiglu(x1, h2, w_gate_b[l], w_up_b[l], w_down_b[l], tm=tm_s, ff_chunk=ff_chunk)
        outs[3].append(k_new.reshape(nseq, dec_seq, n_heads, head_dim))
        outs[4].append(v_new.reshape(nseq, dec_seq, n_heads, head_dim))
        outs[5].append(jnp.swapaxes(new_pool, 0, 1))
        outs[6].append(smaj(vb))
    y_prompt = xp.reshape(bsz, seq, d_model)
    y_sample = jnp.swapaxes(xs.reshape(dec_seq, nseq, d_model), 0, 1)
    stacked = [jnp.stack(o) for o in outs]
    stacked[0] = jnp.transpose(stacked[0], (0, 1, 4, 2, 3))
    stacked[1] = jnp.transpose(stacked[1], (0, 1, 4, 2, 3))
    return (y_prompt, y_sample) + tuple(stacked)
```

```python
import functools

import jax
import jax.numpy as jnp
from jax import lax
from jax.experimental import pallas as pl
from jax.experimental.pallas import tpu as pltpu

F32 = jnp.float32
BF16 = jnp.bfloat16

NORM_EPS = 1e-6
POOL_WINDOWS = (2, 4, 8, 16)
POOL_BUF = 15
GROUP_DIM = 64
N_GROUPS = 4
MIX_DIM = N_GROUPS * GROUP_DIM
CHUNK = 128
HEAD_DIM = 64
LANES = 128
LOG2_E = 1.4426950408889634
PROMPT_KEY_BLOCK = 256
UNROLL_KEY_BLOCKS = 4
VMEM_LIMIT_BYTES = 56 * 1024 * 1024


def _cparams(n_grid_dims):
    return pltpu.CompilerParams(
        dimension_semantics=("arbitrary",) * n_grid_dims,
        vmem_limit_bytes=VMEM_LIMIT_BYTES)


def _dot(a, b):
    return jnp.dot(a, b, preferred_element_type=F32)


def _dot_nt(a, b):
    return lax.dot_general(a, b, (((1,), (1,)), ((), ())), preferred_element_type=F32)


def _rms_rows(x, gain):
    ms = jnp.mean(x * x, axis=-1, keepdims=True)
    return (x * lax.rsqrt(ms + NORM_EPS)) * gain


def _sigmoid(x):
    return 1.0 / (1.0 + jnp.exp(-x))


def _in_proj_body(x_ref, n1_ref, w_ref, qg_ref, kg_ref, hmean_ref,
                  a_ref, u_ref, vb_ref, q_ref, k_ref, v_ref, g_ref,
                  *, sb_dim, d_model, transpose_kv):
    hb = _rms_rows(x_ref[...], n1_ref[...]).astype(BF16)

    def proj(c0, width):
        return _dot(hb, w_ref[:, c0:c0 + width])

    def head_rms(y, gain):
        ms = _dot((y * y).astype(BF16), hmean_ref[...])
        return (y * lax.rsqrt(ms + NORM_EPS)) * gain

    def store_kv(ref, y):
        if transpose_kv:
            ref[...] = jnp.transpose(y).reshape(ref.shape)
        else:
            ref[...] = y

    c = 0
    a_ref[...] = proj(c, MIX_DIM)
    c += MIX_DIM
    u_ref[...] = proj(c, MIX_DIM)
    c += MIX_DIM
    vb_ref[...] = proj(c, MIX_DIM)
    c += MIX_DIM
    q_ref[...] = head_rms(proj(c, sb_dim), qg_ref[...]).astype(BF16)
    c += sb_dim
    store_kv(k_ref, head_rms(proj(c, sb_dim), kg_ref[...]))
    c += sb_dim
    store_kv(v_ref, proj(c, sb_dim))
    c += sb_dim
    for j in range(3):
        g_ref[:, j * d_model:(j + 1) * d_model] = _sigmoid(proj(c, d_model)).astype(BF16)
        c += d_model


def _in_proj(x, norm1, w_in, q_gain_s, k_gain_t, hmean, *, tm, seq_len=None):
    n, d_model = x.shape
    sb_dim = hmean.shape[0]
    row = lambda width: pl.BlockSpec((tm, width), lambda i: (i, 0))
    full = lambda arr: pl.BlockSpec(arr.shape, lambda i: (0,) * arr.ndim)
    if seq_len is None:
        kv_shape = jax.ShapeDtypeStruct((n, sb_dim), F32)
        kv_spec = row(sb_dim)
    else:
        tiles = seq_len // tm
        n_heads = sb_dim // HEAD_DIM
        kv_shape = jax.ShapeDtypeStruct((n // seq_len, n_heads, HEAD_DIM, seq_len), F32)
        kv_spec = pl.BlockSpec((None, n_heads, HEAD_DIM, tm), lambda i: (i // tiles, 0, 0, i % tiles))
    out_shapes = (
        jax.ShapeDtypeStruct((n, MIX_DIM), F32),
        jax.ShapeDtypeStruct((n, MIX_DIM), F32),
        jax.ShapeDtypeStruct((n, MIX_DIM), F32),
        jax.ShapeDtypeStruct((n, sb_dim), BF16),
        kv_shape,
        kv_shape,
        jax.ShapeDtypeStruct((n, 3 * d_model), BF16),
    )
    out_specs = [row(MIX_DIM)] * 3 + [row(sb_dim), kv_spec, kv_spec, row(3 * d_model)]
    return pl.pallas_call(
        functools.partial(_in_proj_body, sb_dim=sb_dim, d_model=d_model, transpose_kv=seq_len is not None),
        grid=(n // tm,),
        in_specs=[row(d_model), full(norm1), full(w_in), full(q_gain_s), full(k_gain_t), full(hmean)],
        out_specs=out_specs,
        out_shape=out_shapes,
        compiler_params=_cparams(1),
        name="in_proj",
    )(x, norm1, w_in, q_gain_s, k_gain_t, hmean)


def _group_of_lane(shape):
    return lax.broadcasted_iota(jnp.int32, shape, len(shape) - 1) // GROUP_DIM


def _mix_prompt_body(a_ref, halo_ref, u_ref, vb_ref, ws_ref, bsf_ref, wpool_ref, pscale_ref,
                     ao_ref, bo_ref, e_ref, s2_ref, s4_ref, s8_ref, *, rows):
    c = pl.program_id(1)
    end = 32 + rows
    a = a_ref[...]
    e_ref[0:16, :] = jnp.zeros((16, MIX_DIM), F32)
    e_ref[16:32, :] = jnp.where(c > 0, halo_ref[...], 0.0)
    e_ref[32:end, :] = a
    s2_ref[8:end, :] = e_ref[8:end, :] + e_ref[7:end - 1, :]
    s4_ref[16:end, :] = s2_ref[16:end, :] + s2_ref[14:end - 2, :]
    s8_ref[24:end, :] = s4_ref[24:end, :] + s4_ref[20:end - 4, :]
    s16 = s8_ref[32:end, :] + s8_ref[24:end - 8, :]
    grp = _group_of_lane((rows, MIX_DIM))
    row = lax.broadcasted_iota(jnp.int32, (rows, MIX_DIM), 0)
    s = jnp.where(grp == 0, s2_ref[32:end, :],
                  jnp.where(grp == 1, s4_ref[32:end, :],
                            jnp.where(grp == 2, s8_ref[32:end, :], s16)))
    window = jnp.left_shift(2, grp)
    cnt = jnp.minimum(c * rows + row + 1, window).astype(F32)
    d = s / cnt - a
    ao_ref[...] = (_dot(d.astype(BF16), wpool_ref[...]) * pscale_ref[...]).astype(BF16)
    r128 = lax.broadcasted_iota(jnp.int32, (CHUNK, CHUNK), 0)
    c128 = lax.broadcasted_iota(jnp.int32, (CHUNK, CHUNK), 1)
    wg = [jnp.where(c128 <= r128, ws_ref[g], jnp.zeros((), BF16)) for g in range(N_GROUPS)]
    grp_c = _group_of_lane((CHUNK, MIX_DIM))
    for r0 in range(0, rows, CHUNK):
        vbb = vb_ref[r0:r0 + CHUNK, :].astype(BF16)
        o = jnp.zeros((CHUNK, MIX_DIM), F32)
        for g in range(N_GROUPS):
            o = jnp.where(grp_c == g, _dot(wg[g], vbb), o)
        bo_ref[r0:r0 + CHUNK, :] = (u_ref[r0:r0 + CHUNK, :] * (o + bsf_ref[...])).astype(BF16)


def _mix_prompt(a, u, vb, ws, bsf, wpool_bd, pscale, *, rows):
    bsz, t, _ = a.shape
    assert rows % CHUNK == 0 and t % rows == 0
    blk = pl.BlockSpec((None, rows, MIX_DIM), lambda b, c: (b, c, 0))
    halo = pl.BlockSpec((None, 16, MIX_DIM), lambda b, c: (b, jnp.maximum(c * (rows // 16) - 1, 0), 0))
    full = lambda arr: pl.BlockSpec(arr.shape, lambda b, c: (0,) * arr.ndim)
    out = jax.ShapeDtypeStruct((bsz, t, MIX_DIM), BF16)
    return pl.pallas_call(
        functools.partial(_mix_prompt_body, rows=rows),
        grid=(bsz, t // rows),
        in_specs=[blk, halo, blk, blk, full(ws), full(bsf), full(wpool_bd), full(pscale)],
        out_specs=[blk, blk],
        out_shape=(out, out),
        scratch_shapes=[pltpu.VMEM((32 + rows, MIX_DIM), F32)] * 4,
        compiler_params=_cparams(2),
        name="mix_prompt",
    )(a, a, u, vb, ws, bsf, wpool_bd, pscale)


def _mix_sample_body(a_ref, prev_ref, u_ref, vb_ref, wrow_ref, bsf_ref, wpool_ref, pscale_ref,
                     ao_ref, bo_ref, np_ref, *, dec_seq, first_pos):
    nseq = a_ref.shape[1]
    grp = _group_of_lane((nseq, MIX_DIM))
    e = [prev_ref[j] for j in range(POOL_BUF)] + [a_ref[t] for t in range(dec_seq)]
    n_e = len(e)
    sums = []
    prev_level, span = e, 1
    for _ in POOL_WINDOWS:
        level = [None] * n_e
        for j in range(n_e):
            if j - span >= 0 and prev_level[j] is not None and prev_level[j - span] is not None:
                level[j] = prev_level[j] + prev_level[j - span]
        sums.append(level)
        prev_level, span = level, span * 2
    for t in range(dec_seq):
        j = POOL_BUF + t
        mean = None
        for k, w in enumerate(POOL_WINDOWS):
            m = sums[k][j] / float(min(first_pos + t + 1, w))
            mean = m if mean is None else jnp.where(grp == k, m, mean)
        d = mean - e[j]
        ao_ref[t] = (_dot(d.astype(BF16), wpool_ref[...]) * pscale_ref[...]).astype(BF16)
        o = bsf_ref[t:t + 1, :]
        for s in range(t + 1):
            o = o + wrow_ref[t, s:s + 1, :] * vb_ref[s]
        bo_ref[t] = (u_ref[t] * o).astype(BF16)
    for j in range(POOL_BUF):
        np_ref[j] = e[n_e - POOL_BUF + j]


def _mix_sample(a, prev, u, vb, wrow, bsf, wpool_bd, pscale, *, first_pos):
    dec_seq, nseq, _ = a.shape
    full = lambda arr: pl.BlockSpec(arr.shape, lambda i: (0,) * arr.ndim)
    args = (a, prev, u, vb, wrow, bsf, wpool_bd, pscale)
    out_shapes = (jax.ShapeDtypeStruct((dec_seq, nseq, MIX_DIM), BF16),
                  jax.ShapeDtypeStruct((dec_seq, nseq, MIX_DIM), BF16),
                  jax.ShapeDtypeStruct((POOL_BUF, nseq, MIX_DIM), F32))
    return pl.pallas_call(
        functools.partial(_mix_sample_body, dec_seq=dec_seq, first_pos=first_pos),
        grid=(1,),
        in_specs=[full(x) for x in args],
        out_specs=[full(s) for s in out_shapes],
        out_shape=out_shapes,
        compiler_params=_cparams(1),
        name="mix_sample",
    )(*args)


def _sb_logs(z):
    neg_part = jnp.minimum(z, 0.0)
    neg_rest = neg_part - z
    softplus = jnp.log2(1.0 + jnp.exp2(neg_part + neg_rest))
    return neg_part - softplus, neg_rest - softplus


def _attn_prompt_body(bias_ref, q_ref, k_ref, v_ref, su_ref, o_ref, carry_ref, acc_ref, qa_ref, *, tq, kb):
    hp = pl.program_id(1)
    i = pl.program_id(2)
    n_diag = tq // kb
    su = su_ref[...]
    zeros = jnp.zeros((HEAD_DIM, kb), BF16)
    carry_ref[...] = jnp.zeros(carry_ref.shape, F32)
    acc_ref[...] = jnp.zeros(acc_ref.shape, F32)
    lane = lax.broadcasted_iota(jnp.int32, (1, 2 * kb), 1)
    bias_row = jnp.where(lane < kb, bias_ref[hp * 2], bias_ref[hp * 2 + 1])
    b_hi = bias_row.astype(BF16).astype(F32)
    b_mid = (bias_row - b_hi).astype(BF16).astype(F32)
    b_lo = (bias_row - b_hi) - b_mid
    chan = lax.broadcasted_iota(jnp.int32, (LANES, 2 * kb), 0)
    bias_rows = jnp.where(chan == 0, b_hi, jnp.where(chan == 1, b_mid, jnp.where(chan == 2, b_lo, 0.0))).astype(BF16)
    qlane = lax.broadcasted_iota(jnp.int32, (tq, LANES), 1)
    qa_ref[:, 0:LANES] = q_ref[...]
    qa_ref[:, LANES:2 * LANES] = jnp.where(qlane < 3, 1.0, 0.0).astype(BF16)

    def block_diag(ref, off):
        t = ref[:, :, pl.ds(off, kb)].astype(BF16)
        return jnp.concatenate([jnp.concatenate([t[0], zeros], axis=1),
                                jnp.concatenate([zeros, t[1]], axis=1)], axis=0)

    def step(j, row0, row1, mask):
        off = pl.multiple_of(j * kb, kb)
        z = _dot(qa_ref[row0:row1, :], jnp.concatenate([block_diag(k_ref, off), bias_rows], axis=0))
        log_beta, log_rem = _sb_logs(z)
        if mask is not None:
            log_rem = jnp.where(mask, log_rem, 0.0)
        carries, suffix = [], []
        for hh in range(2):
            lr = log_rem[:, hh * kb:(hh + 1) * kb]
            suffix.append(_dot(lr.astype(BF16), su))
            carry = carry_ref[hh, row0:row1, :]
            carries.append(jnp.concatenate([carry] * (kb // LANES), axis=1))
            carry_ref[hh, row0:row1, :] = carry + jnp.sum(lr, axis=-1, keepdims=True)
        att = jnp.exp2(log_beta + jnp.concatenate(suffix, axis=1) + jnp.concatenate(carries, axis=1))
        if mask is not None:
            att = jnp.where(mask, att, 0.0)
        acc_ref[row0:row1, :] += _dot_nt(att.astype(BF16), block_diag(v_ref, off))

    first = i * n_diag
    key_in_block = lax.broadcasted_iota(jnp.int32, (kb, 2 * kb), 1) % kb
    triangle = key_in_block < lax.broadcasted_iota(jnp.int32, (kb, 2 * kb), 0)
    for dj in reversed(range(n_diag)):
        step(first + dj, dj * kb, (dj + 1) * kb, triangle)
        if (dj + 1) * kb < tq:
            step(first + dj, (dj + 1) * kb, tq, None)

    def run(top, n_blocks, unroll):
        def body(t, c):
            for dj in range(unroll):
                step(top - 1 - dj - unroll * t, 0, tq, None)
            return c
        lax.fori_loop(0, n_blocks // unroll, body, 0)

    n_main = (first // UNROLL_KEY_BLOCKS) * UNROLL_KEY_BLOCKS
    run(first, n_main, UNROLL_KEY_BLOCKS)
    if UNROLL_KEY_BLOCKS != n_diag:
        assert UNROLL_KEY_BLOCKS % n_diag == 0
        run(first - n_main, first - n_main, n_diag)
    o_ref[...] = acc_ref[...].astype(o_ref.dtype)


def _attn_prompt(q, kt, vt, bias, suffix_upper, *, tq):
    bsz, t, sb_dim = q.shape
    n_pairs = sb_dim // LANES
    kb = suffix_upper.shape[0]
    qspec = pl.BlockSpec((None, tq, LANES), lambda b, hp, i: (b, i, hp))
    kspec = pl.BlockSpec((None, 2, HEAD_DIM, t), lambda b, hp, i: (b, hp, 0, 0))
    return pl.pallas_call(
        functools.partial(_attn_prompt_body, tq=tq, kb=kb),
        grid=(bsz, n_pairs, t // tq),
        in_specs=[pl.BlockSpec(memory_space=pltpu.SMEM), qspec, kspec, kspec,
                  pl.BlockSpec(suffix_upper.shape, lambda b, hp, i: (0, 0))],
        out_specs=qspec,
        out_shape=jax.ShapeDtypeStruct((bsz, t, sb_dim), BF16),
        scratch_shapes=[pltpu.VMEM((2, tq, LANES), F32), pltpu.VMEM((tq, LANES), F32),
                        pltpu.VMEM((tq, 2 * LANES), BF16)],
        compiler_params=_cparams(3),
        name="attn_prompt",
    )(bias, q, kt, vt, suffix_upper)


def _attn_sample_body(pt_ref, bias_ref, q_ref, kn_ref, vn_ref, *rest, n_heads, dec_seq, n_pages, seqs):
    del pt_ref
    so_ref, o_ref = rest[2 * seqs * n_pages:]
    biasb = jnp.concatenate([jnp.full((dec_seq, LANES), bias_ref[h], F32) for h in range(n_heads)], axis=0)
    for s in range(seqs):
        kpages = rest[s * n_pages:(s + 1) * n_pages]
        vpages = rest[(seqs + s) * n_pages:(seqs + s + 1) * n_pages]
        o_ref[s] = _attn_sample_one(q_ref[s], kn_ref[s], vn_ref[s], kpages, vpages, so_ref[...], biasb,
                                    n_heads=n_heads, dec_seq=dec_seq)


def _attn_sample_one(q, k_new, v_new, kpage_refs, vpage_refs, suffix_ones, biasb, *, n_heads, dec_seq):
    n_pages = len(kpage_refs)
    rows = n_heads * dec_seq
    sb_dim = n_heads * HEAD_DIM
    lane = lax.broadcasted_iota(jnp.int32, (dec_seq, sb_dim), 1)
    in_head = [(lane >= h * HEAD_DIM) & (lane < (h + 1) * HEAD_DIM) for h in range(n_heads)]
    qbd = jnp.concatenate([jnp.where(in_head[h], q, 0.0) for h in range(n_heads)], axis=0).astype(BF16)
    pad = jnp.zeros((LANES - dec_seq, sb_dim), F32)
    kn = jnp.concatenate([k_new, pad], axis=0).astype(BF16)
    vn = jnp.concatenate([v_new, pad], axis=0).astype(BF16)
    kidx = lax.broadcasted_iota(jnp.int32, (rows, LANES), 1)
    trow = lax.broadcasted_iota(jnp.int32, (rows, LANES), 0) % dec_seq
    new_mask = kidx < trow
    pages = list(reversed(range(n_pages)))
    z = jnp.concatenate([_dot_nt(qbd, kn) + biasb]
                        + [_dot(qbd, kpage_refs[pg][...].astype(BF16)) + biasb for pg in pages], axis=0)
    log_beta, log_rem = _sb_logs(z)
    log_rem = jnp.concatenate([jnp.where(new_mask, log_rem[:rows], 0.0), log_rem[rows:]], axis=0)
    cs = _dot(log_rem.astype(BF16), suffix_ones)
    carries = [jnp.zeros((rows, LANES), F32)]
    for p in range(n_pages):
        carries.append(carries[-1] + cs[p * rows:(p + 1) * rows, LANES:])
    att = jnp.exp2(log_beta + cs[:, :LANES] + jnp.concatenate(carries, axis=0))
    acc = _dot(jnp.where(new_mask, att[:rows], 0.0).astype(BF16), vn)
    for p, pg in enumerate(pages):
        acc = acc + _dot_nt(att[(p + 1) * rows:(p + 2) * rows].astype(BF16), vpage_refs[pg][...].astype(BF16))
    out = jnp.zeros((dec_seq, sb_dim), F32)
    for h in range(n_heads):
        out = jnp.where(in_head[h], acc[h * dec_seq:(h + 1) * dec_seq, :], out)
    return out


def _attn_sample(page_table, bias, q, k_new, v_new, cache_kt, cache_vt, suffix_ones, *, layer):
    nseq, dec_seq, sb_dim = q.shape
    n_heads = sb_dim // HEAD_DIM
    n_pages = page_table.shape[1]
    page_size = cache_kt.shape[-1]
    assert page_size == LANES
    seqs = 2 if nseq % 2 == 0 else 1
    seq_spec = pl.BlockSpec((seqs, dec_seq, sb_dim), lambda n, pt: (n, 0, 0))

    def page_spec(s, pg):
        return pl.BlockSpec((None, None, sb_dim, page_size), lambda n, pt: (layer, pt[n * seqs + s, pg], 0, 0))

    page_specs = [page_spec(s, pg) for s in range(seqs) for pg in range(n_pages)]
    grid_spec = pltpu.PrefetchScalarGridSpec(
        num_scalar_prefetch=1,
        grid=(nseq // seqs,),
        in_specs=[pl.BlockSpec(memory_space=pltpu.SMEM), seq_spec, seq_spec, seq_spec]
        + page_specs + page_specs
        + [pl.BlockSpec(suffix_ones.shape, lambda n, pt: (0, 0))],
        out_specs=seq_spec,
    )
    return pl.pallas_call(
        functools.partial(_attn_sample_body, n_heads=n_heads, dec_seq=dec_seq, n_pages=n_pages, seqs=seqs),
        grid_spec=grid_spec,
        out_shape=jax.ShapeDtypeStruct((nseq, dec_seq, sb_dim), F32),
        compiler_params=_cparams(1),
        name="attn_sample",
    )(page_table, bias, q, k_new, v_new, *([cache_kt] * (seqs * n_pages)), *([cache_vt] * (seqs * n_pages)),
      suffix_ones)


def _merge_ffn_body(x_ref, a_ref, b_ref, c_ref, g_ref, wpa_ref, wpb_ref, wpc_ref, wo_ref, n2_ref,
                    wg_ref, wu_ref, wd_ref, o_ref, *, d_model, ff_chunk):
    merged = (g_ref[:, 0:d_model].astype(F32) * _dot(a_ref[...], wpa_ref[...])
              + g_ref[:, d_model:2 * d_model].astype(F32) * _dot(b_ref[...], wpb_ref[...])
              + g_ref[:, 2 * d_model:3 * d_model].astype(F32) * _dot(c_ref[...], wpc_ref[...]))
    acc = x_ref[...] + _dot(merged.astype(BF16), wo_ref[...])
    h = _rms_rows(acc, n2_ref[...]).astype(BF16)
    for c0 in range(0, wg_ref.shape[1], ff_chunk):
        gate = _dot(h, wg_ref[:, c0:c0 + ff_chunk])
        up = _dot(h, wu_ref[:, c0:c0 + ff_chunk])
        act = (gate * _sigmoid(gate) * up).astype(BF16)
        acc = acc + _dot(act, wd_ref[c0:c0 + ff_chunk, :])
    o_ref[...] = acc


def _merge_ffn(x, a_out, b_out, c_out, gates, w_pa, w_pb, w_pc, w_o, norm2, w_gate, w_up, w_down,
               *, tm, ff_chunk):
    n, d_model = x.shape
    row = lambda arr: pl.BlockSpec((tm, arr.shape[1]), lambda i: (i, 0))
    full = lambda arr: pl.BlockSpec(arr.shape, lambda i: (0,) * arr.ndim, pipeline_mode=pl.Buffered(1))
    rows = (x, a_out, b_out, c_out, gates)
    consts = (w_pa, w_pb, w_pc, w_o, norm2, w_gate, w_up, w_down)
    return pl.pallas_call(
        functools.partial(_merge_ffn_body, d_model=d_model, ff_chunk=ff_chunk),
        grid=(n // tm,),
        in_specs=[row(r) for r in rows] + [full(c) for c in consts],
        out_specs=row(x),
        out_shape=jax.ShapeDtypeStruct((n, d_model), F32),
        compiler_params=_cparams(1),
        name="merge_ffn",
    )(*rows, *consts)


def _row_tile(n, target):
    tm = min(n, target)
    assert n % tm == 0
    return tm


def _ff_chunk(d_ff):
    for c in (512, 256, 128):
        if d_ff % c == 0:
            return c
    return d_ff


def kernel(x_prompt, x_sample, cache_k, cache_v, state_pool, page_table, norm1, w_in, w_pool, pool_scale,
           w_s, b_s, q_gain, k_gain, sb_bias, w_pa, w_pb, w_pc, w_o, norm2, w_gate, w_up, w_down):
    depth = w_in.shape[0]
    bsz, seq, d_model = x_prompt.shape
    nseq, dec_seq, _ = x_sample.shape
    n_pool, page_size, n_heads, head_dim = cache_k.shape[1:]
    n_pages = page_table.shape[1]
    past_len = n_pages * page_size
    sb_dim = n_heads * head_dim
    d_ff = w_gate.shape[2]
    assert head_dim == HEAD_DIM and w_pool.shape[1] == N_GROUPS and w_s.shape[2] == CHUNK
    assert seq % CHUNK == 0 and past_len % CHUNK == 0 and dec_seq <= CHUNK

    lane = jnp.arange(sb_dim)
    hmean = jnp.where((lane[:, None] // HEAD_DIM) == (lane[None, :] // HEAD_DIM), 1.0 / HEAD_DIM, 0.0).astype(BF16)
    kidx = jnp.arange(LANES)
    suffix_ones = jnp.concatenate(
        [(kidx[:, None] > kidx[None, :]).astype(BF16), jnp.ones((LANES, LANES), BF16)], axis=1)
    kb_p = min(PROMPT_KEY_BLOCK, seq)
    kidx_p = jnp.arange(kb_p)
    suffix_upper = (kidx_p[:, None] > kidx_p[None, :]).astype(BF16)
    w_in_b = w_in.astype(BF16)
    w_pa_b, w_pb_b, w_pc_b, w_o_b = (w.astype(BF16) for w in (w_pa, w_pb, w_pc, w_o))
    w_gate_b, w_up_b, w_down_b = (w.astype(BF16) for w in (w_gate, w_up, w_down))
    w_s_b = w_s.astype(BF16)
    grp = jnp.arange(MIX_DIM) // GROUP_DIM
    wpool_bd = jnp.where(grp[None, :, None] == grp[None, None, :],
                         jnp.tile(w_pool.reshape(depth, MIX_DIM, GROUP_DIM), (1, 1, N_GROUPS)), 0.0).astype(BF16)
    bsf = jnp.repeat(jnp.swapaxes(b_s, 1, 2), GROUP_DIM, axis=2)
    wrow = jnp.repeat(jnp.transpose(w_s[:, :, :dec_seq, :dec_seq], (0, 2, 3, 1)), GROUP_DIM, axis=3)
    qg = jnp.tile(q_gain, (1, n_heads)) * (HEAD_DIM ** -0.5 * LOG2_E)
    bias2 = sb_bias * LOG2_E
    kg = jnp.tile(k_gain, (1, n_heads))
    cache_kt = jnp.transpose(cache_k, (0, 1, 3, 4, 2)).reshape(depth, n_pool, sb_dim, page_size)
    cache_vt = jnp.transpose(cache_v, (0, 1, 3, 4, 2)).reshape(depth, n_pool, sb_dim, page_size)

    n_p = bsz * seq
    n_s = nseq * dec_seq
    tm_p = _row_tile(n_p, 512)
    tm_s = _row_tile(n_s, 512)
    tq = _row_tile(seq, 512)
    ff_chunk = _ff_chunk(d_ff)

    xp = x_prompt.reshape(n_p, d_model)
    xs = jnp.swapaxes(x_sample, 0, 1).reshape(n_s, d_model)
    outs = [[] for _ in range(7)]
    for l in range(depth):
        lw = dict(norm1=norm1[l][None], w_in=w_in_b[l], qg=qg[l][None], kg=kg[l][None])
        a, u, vb, q, kt, vt, gates = _in_proj(
            xp, lw["norm1"], lw["w_in"], lw["qg"], lw["kg"], hmean, tm=tm_p, seq_len=seq)
        a3 = a.reshape(bsz, seq, MIX_DIM)
        a_out, b_out = _mix_prompt(a3, u.reshape(bsz, seq, MIX_DIM), vb.reshape(bsz, seq, MIX_DIM),
                                   w_s_b[l], bsf[l], wpool_bd[l], pool_scale[l][None], rows=tq)
        c_out = _attn_prompt(q.reshape(bsz, seq, sb_dim), kt, vt, bias2[l], suffix_upper, tq=tq)
        xp = _merge_ffn(xp, a_out.reshape(n_p, MIX_DIM), b_out.reshape(n_p, MIX_DIM), c_out.reshape(n_p, sb_dim),
                        gates, w_pa_b[l], w_pb_b[l], w_pc_b[l], w_o_b[l], norm2[l][None],
                        w_gate_b[l], w_up_b[l], w_down_b[l], tm=tm_p, ff_chunk=ff_chunk)
        outs[0].append(kt)
        outs[1].append(vt)
        outs[2].append(a3[:, seq - POOL_BUF:, :])
        a, u, vb, q, k32, v32, gates = _in_proj(
            xs, lw["norm1"], lw["w_in"], lw["qg"], lw["kg"], hmean, tm=tm_s)
        tmaj = lambda arr: arr.reshape(dec_seq, nseq, arr.shape[-1])
        smaj = lambda arr: jnp.swapaxes(tmaj(arr), 0, 1)
        a_out, b_out, new_pool = _mix_sample(
            tmaj(a), jnp.swapaxes(state_pool[l], 0, 1), tmaj(u), tmaj(vb),
            wrow[l], bsf[l], wpool_bd[l], pool_scale[l][None], first_pos=past_len)
        k_new = smaj(k32)
        v_new = smaj(v32)
        c_out = _attn_sample(page_table, bias2[l], smaj(q.astype(F32)), k_new, v_new,
                             cache_kt, cache_vt, suffix_ones, layer=l)
        c_out = jnp.swapaxes(c_out, 0, 1).reshape(n_s, sb_dim).astype(BF16)
        xs = _merge_ffn(xs, a_out.reshape(n_s, MIX_DIM), b_out.reshape(n_s, MIX_DIM), c_out,
                        gates, w_pa_b[l], w_pb_b[l], w_pc_b[l], w_o_b[l], norm2[l][None],
                        w_gate_b[l], w_up_b[l], w_down_b[l], tm=tm_s, ff_chunk=ff_chunk)
        outs[3].append(k_new.reshape(nseq, dec_seq, n_heads, head_dim))
        outs[4].append(v_new.reshape(nseq, dec_seq, n_heads, head_dim))
        outs[5].append(jnp.swapaxes(new_pool, 0, 1))
        outs[6].append(smaj(vb))
    y_prompt = xp.reshape(bsz, seq, d_model)
    y_sample = jnp.swapaxes(xs.reshape(dec_seq, nseq, d_model), 0, 1)
    stacked = [jnp.stack(o) for o in outs]
    stacked[0] = jnp.transpose(stacked[0], (0, 1, 4, 2, 3))
    stacked[1] = jnp.transpose(stacked[1], (0, 1, 4, 2, 3))
    return (y_prompt, y_sample) + tuple(stacked)
```

```python
import functools

import jax
import jax.numpy as jnp
from jax import lax
from jax.experimental import pallas as pl
from jax.experimental.pallas import tpu as pltpu

F32 = jnp.float32
BF16 = jnp.bfloat16

NORM_EPS = 1e-6
POOL_WINDOWS = (2, 4, 8, 16)
POOL_BUF = 15
GROUP_DIM = 64
N_GROUPS = 4
MIX_DIM = N_GROUPS * GROUP_DIM
CHUNK = 128
HEAD_DIM = 64
LANES = 128
LOG2_E = 1.4426950408889634
PROMPT_KEY_BLOCK = 256
UNROLL_KEY_BLOCKS = 4
VMEM_LIMIT_BYTES = 56 * 1024 * 1024


def _cparams(n_grid_dims):
    return pltpu.CompilerParams(
        dimension_semantics=("arbitrary",) * n_grid_dims,
        vmem_limit_bytes=VMEM_LIMIT_BYTES)


def _fixed_spec(arr, layer=None, **kwargs):
    if layer is None:
        return pl.BlockSpec(arr.shape, lambda i: (0,) * arr.ndim, **kwargs)
    return pl.BlockSpec((None,) + arr.shape[1:], lambda i: (layer,) + (0,) * (arr.ndim - 1), **kwargs)


def _dot(a, b):
    return jnp.dot(a, b, preferred_element_type=F32)


def _dot_nt(a, b):
    return lax.dot_general(a, b, (((1,), (1,)), ((), ())), preferred_element_type=F32)


def _rms_rows(x, gain):
    ms = jnp.mean(x * x, axis=-1, keepdims=True)
    return (x * lax.rsqrt(ms + NORM_EPS)) * gain


def _sigmoid(x):
    return 1.0 / (1.0 + jnp.exp(-x))


def _in_proj_body(x_ref, n1_ref, w_ref, qg_ref, kg_ref, hmean_ref,
                  a_ref, u_ref, vb_ref, q_ref, k_ref, v_ref, g_ref,
                  *, sb_dim, d_model, transpose_kv):
    hb = _rms_rows(x_ref[...], n1_ref[...]).astype(BF16)

    def proj(c0, width):
        return _dot(hb, w_ref[:, c0:c0 + width])

    def head_rms(y, gain):
        ms = _dot((y * y).astype(BF16), hmean_ref[...])
        return (y * lax.rsqrt(ms + NORM_EPS)) * gain

    def store_kv(ref, y):
        if transpose_kv:
            ref[...] = jnp.transpose(y).reshape(ref.shape)
        else:
            ref[...] = y

    c = 0
    a_ref[...] = proj(c, MIX_DIM)
    c += MIX_DIM
    u_ref[...] = proj(c, MIX_DIM)
    c += MIX_DIM
    vb_ref[...] = proj(c, MIX_DIM)
    c += MIX_DIM
    q_ref[...] = head_rms(proj(c, sb_dim), qg_ref[...]).astype(BF16)
    c += sb_dim
    store_kv(k_ref, head_rms(proj(c, sb_dim), kg_ref[...]))
    c += sb_dim
    store_kv(v_ref, proj(c, sb_dim))
    c += sb_dim
    for j in range(3):
        g_ref[:, j * d_model:(j + 1) * d_model] = _sigmoid(proj(c, d_model)).astype(BF16)
        c += d_model


def _in_proj(x, norm1, w_in_all, q_gain_s, k_gain_t, hmean, *, layer, tm, seq_len=None):
    n, d_model = x.shape
    sb_dim = hmean.shape[0]
    row = lambda width: pl.BlockSpec((tm, width), lambda i: (i, 0))
    full = _fixed_spec
    if seq_len is None:
        kv_shape = jax.ShapeDtypeStruct((n, sb_dim), F32)
        kv_spec = row(sb_dim)
    else:
        tiles = seq_len // tm
        n_heads = sb_dim // HEAD_DIM
        kv_shape = jax.ShapeDtypeStruct((n // seq_len, n_heads, HEAD_DIM, seq_len), F32)
        kv_spec = pl.BlockSpec((None, n_heads, HEAD_DIM, tm), lambda i: (i // tiles, 0, 0, i % tiles))
    out_shapes = (
        jax.ShapeDtypeStruct((n, MIX_DIM), F32),
        jax.ShapeDtypeStruct((n, MIX_DIM), F32),
        jax.ShapeDtypeStruct((n, MIX_DIM), F32),
        jax.ShapeDtypeStruct((n, sb_dim), BF16),
        kv_shape,
        kv_shape,
        jax.ShapeDtypeStruct((n, 3 * d_model), BF16),
    )
    out_specs = [row(MIX_DIM)] * 3 + [row(sb_dim), kv_spec, kv_spec, row(3 * d_model)]
    return pl.pallas_call(
        functools.partial(_in_proj_body, sb_dim=sb_dim, d_model=d_model, transpose_kv=seq_len is not None),
        grid=(n // tm,),
        in_specs=[row(d_model), full(norm1), full(w_in_all, layer), full(q_gain_s), full(k_gain_t), full(hmean)],
        out_specs=out_specs,
        out_shape=out_shapes,
        compiler_params=_cparams(1),
        name="in_proj",
    )(x, norm1, w_in_all, q_gain_s, k_gain_t, hmean)


def _group_of_lane(shape):
    return lax.broadcasted_iota(jnp.int32, shape, len(shape) - 1) // GROUP_DIM


def _mix_prompt_body(a_ref, halo_ref, u_ref, vb_ref, ws_ref, bsf_ref, wpool_ref, pscale_ref,
                     ao_ref, bo_ref, e_ref, s2_ref, s4_ref, s8_ref, *, rows):
    c = pl.program_id(1)
    end = 32 + rows
    a = a_ref[...]
    e_ref[0:16, :] = jnp.zeros((16, MIX_DIM), F32)
    e_ref[16:32, :] = jnp.where(c > 0, halo_ref[...], 0.0)
    e_ref[32:end, :] = a
    s2_ref[8:end, :] = e_ref[8:end, :] + e_ref[7:end - 1, :]
    s4_ref[16:end, :] = s2_ref[16:end, :] + s2_ref[14:end - 2, :]
    s8_ref[24:end, :] = s4_ref[24:end, :] + s4_ref[20:end - 4, :]
    s16 = s8_ref[32:end, :] + s8_ref[24:end - 8, :]
    grp = _group_of_lane((rows, MIX_DIM))
    row = lax.broadcasted_iota(jnp.int32, (rows, MIX_DIM), 0)
    s = jnp.where(grp == 0, s2_ref[32:end, :],
                  jnp.where(grp == 1, s4_ref[32:end, :],
                            jnp.where(grp == 2, s8_ref[32:end, :], s16)))
    window = jnp.left_shift(2, grp)
    cnt = jnp.minimum(c * rows + row + 1, window).astype(F32)
    d = s / cnt - a
    ao_ref[...] = (_dot(d.astype(BF16), wpool_ref[...]) * pscale_ref[...]).astype(BF16)
    r128 = lax.broadcasted_iota(jnp.int32, (CHUNK, CHUNK), 0)
    c128 = lax.broadcasted_iota(jnp.int32, (CHUNK, CHUNK), 1)
    wg = [jnp.where(c128 <= r128, ws_ref[g], jnp.zeros((), BF16)) for g in range(N_GROUPS)]
    grp_c = _group_of_lane((CHUNK, MIX_DIM))
    for r0 in range(0, rows, CHUNK):
        vbb = vb_ref[r0:r0 + CHUNK, :].astype(BF16)
        o = jnp.zeros((CHUNK, MIX_DIM), F32)
        for g in range(N_GROUPS):
            o = jnp.where(grp_c == g, _dot(wg[g], vbb), o)
        bo_ref[r0:r0 + CHUNK, :] = (u_ref[r0:r0 + CHUNK, :] * (o + bsf_ref[...])).astype(BF16)


def _mix_prompt(a, u, vb, ws, bsf, wpool_bd, pscale, *, rows):
    bsz, t, _ = a.shape
    assert rows % CHUNK == 0 and t % rows == 0
    blk = pl.BlockSpec((None, rows, MIX_DIM), lambda b, c: (b, c, 0))
    halo = pl.BlockSpec((None, 16, MIX_DIM), lambda b, c: (b, jnp.maximum(c * (rows // 16) - 1, 0), 0))
    full = lambda arr: pl.BlockSpec(arr.shape, lambda b, c: (0,) * arr.ndim)
    out = jax.ShapeDtypeStruct((bsz, t, MIX_DIM), BF16)
    return pl.pallas_call(
        functools.partial(_mix_prompt_body, rows=rows),
        grid=(bsz, t // rows),
        in_specs=[blk, halo, blk, blk, full(ws), full(bsf), full(wpool_bd), full(pscale)],
        out_specs=[blk, blk],
        out_shape=(out, out),
        scratch_shapes=[pltpu.VMEM((32 + rows, MIX_DIM), F32)] * 4,
        compiler_params=_cparams(2),
        name="mix_prompt",
    )(a, a, u, vb, ws, bsf, wpool_bd, pscale)


def _mix_sample_body(a_ref, prev_ref, u_ref, vb_ref, wrow_ref, bsf_ref, wpool_ref, pscale_ref,
                     ao_ref, bo_ref, np_ref, *, dec_seq, first_pos):
    nseq = a_ref.shape[1]
    grp = _group_of_lane((nseq, MIX_DIM))
    e = [prev_ref[j] for j in range(POOL_BUF)] + [a_ref[t] for t in range(dec_seq)]
    n_e = len(e)
    sums = []
    prev_level, span = e, 1
    for _ in POOL_WINDOWS:
        level = [None] * n_e
        for j in range(n_e):
            if j - span >= 0 and prev_level[j] is not None and prev_level[j - span] is not None:
                level[j] = prev_level[j] + prev_level[j - span]
        sums.append(level)
        prev_level, span = level, span * 2
    for t in range(dec_seq):
        j = POOL_BUF + t
        mean = None
        for k, w in enumerate(POOL_WINDOWS):
            m = sums[k][j] / float(min(first_pos + t + 1, w))
            mean = m if mean is None else jnp.where(grp == k, m, mean)
        d = mean - e[j]
        ao_ref[t] = (_dot(d.astype(BF16), wpool_ref[...]) * pscale_ref[...]).astype(BF16)
        o = bsf_ref[t:t + 1, :]
        for s in range(t + 1):
            o = o + wrow_ref[t, s:s + 1, :] * vb_ref[s]
        bo_ref[t] = (u_ref[t] * o).astype(BF16)
    for j in range(POOL_BUF):
        np_ref[j] = e[n_e - POOL_BUF + j]


def _mix_sample(a, prev, u, vb, wrow, bsf, wpool_bd, pscale, *, first_pos):
    dec_seq, nseq, _ = a.shape
    full = lambda arr: pl.BlockSpec(arr.shape, lambda i: (0,) * arr.ndim)
    args = (a, prev, u, vb, wrow, bsf, wpool_bd, pscale)
    out_shapes = (jax.ShapeDtypeStruct((dec_seq, nseq, MIX_DIM), BF16),
                  jax.ShapeDtypeStruct((dec_seq, nseq, MIX_DIM), BF16),
                  jax.ShapeDtypeStruct((POOL_BUF, nseq, MIX_DIM), F32))
    return pl.pallas_call(
        functools.partial(_mix_sample_body, dec_seq=dec_seq, first_pos=first_pos),
        grid=(1,),
        in_specs=[full(x) for x in args],
        out_specs=[full(s) for s in out_shapes],
        out_shape=out_shapes,
        compiler_params=_cparams(1),
        name="mix_sample",
    )(*args)


def _sb_logs(z):
    neg_part = jnp.minimum(z, 0.0)
    neg_rest = neg_part - z
    softplus = jnp.log2(1.0 + jnp.exp2(neg_part + neg_rest))
    return neg_part - softplus, neg_rest - softplus


def _attn_prompt_body(bias_ref, q_ref, k_ref, v_ref, su_ref, o_ref, carry_ref, acc_ref, qa_ref, *, tq, kb):
    hp = pl.program_id(1)
    i = pl.program_id(2)
    n_diag = tq // kb
    su = su_ref[...]
    zeros = jnp.zeros((HEAD_DIM, kb), BF16)
    carry_ref[...] = jnp.zeros(carry_ref.shape, F32)
    acc_ref[...] = jnp.zeros(acc_ref.shape, F32)
    lane = lax.broadcasted_iota(jnp.int32, (1, 2 * kb), 1)
    bias_row = jnp.where(lane < kb, bias_ref[hp * 2], bias_ref[hp * 2 + 1])
    b_hi = bias_row.astype(BF16).astype(F32)
    b_mid = (bias_row - b_hi).astype(BF16).astype(F32)
    b_lo = (bias_row - b_hi) - b_mid
    chan = lax.broadcasted_iota(jnp.int32, (LANES, 2 * kb), 0)
    bias_rows = jnp.where(chan == 0, b_hi, jnp.where(chan == 1, b_mid, jnp.where(chan == 2, b_lo, 0.0))).astype(BF16)
    qlane = lax.broadcasted_iota(jnp.int32, (tq, LANES), 1)
    qa_ref[:, 0:LANES] = q_ref[...]
    qa_ref[:, LANES:2 * LANES] = jnp.where(qlane < 3, 1.0, 0.0).astype(BF16)

    def block_diag(ref, off):
        t = ref[:, :, pl.ds(off, kb)].astype(BF16)
        return jnp.concatenate([jnp.concatenate([t[0], zeros], axis=1),
                                jnp.concatenate([zeros, t[1]], axis=1)], axis=0)

    def step(j, row0, row1, mask):
        off = pl.multiple_of(j * kb, kb)
        z = _dot(qa_ref[row0:row1, :], jnp.concatenate([block_diag(k_ref, off), bias_rows], axis=0))
        log_beta, log_rem = _sb_logs(z)
        if mask is not None:
            log_rem = jnp.where(mask, log_rem, 0.0)
        carries, suffix = [], []
        for hh in range(2):
            lr = log_rem[:, hh * kb:(hh + 1) * kb]
            suffix.append(_dot(lr.astype(BF16), su))
            carry = carry_ref[hh, row0:row1, :]
            carries.append(jnp.concatenate([carry] * (kb // LANES), axis=1))
            carry_ref[hh, row0:row1, :] = carry + jnp.sum(lr, axis=-1, keepdims=True)
        att = jnp.exp2(log_beta + jnp.concatenate(suffix, axis=1) + jnp.concatenate(carries, axis=1))
        if mask is not None:
            att = jnp.where(mask, att, 0.0)
        acc_ref[row0:row1, :] += _dot_nt(att.astype(BF16), block_diag(v_ref, off))

    first = i * n_diag
    key_in_block = lax.broadcasted_iota(jnp.int32, (kb, 2 * kb), 1) % kb
    triangle = key_in_block < lax.broadcasted_iota(jnp.int32, (kb, 2 * kb), 0)
    for dj in reversed(range(n_diag)):
        step(first + dj, dj * kb, (dj + 1) * kb, triangle)
        if (dj + 1) * kb < tq:
            step(first + dj, (dj + 1) * kb, tq, None)

    def run(top, n_blocks, unroll):
        def body(t, c):
            for dj in range(unroll):
                step(top - 1 - dj - unroll * t, 0, tq, None)
            return c
        lax.fori_loop(0, n_blocks // unroll, body, 0)

    n_main = (first // UNROLL_KEY_BLOCKS) * UNROLL_KEY_BLOCKS
    run(first, n_main, UNROLL_KEY_BLOCKS)
    if UNROLL_KEY_BLOCKS != n_diag:
        assert UNROLL_KEY_BLOCKS % n_diag == 0
        run(first - n_main, first - n_main, n_diag)
    o_ref[...] = acc_ref[...].astype(o_ref.dtype)


def _attn_prompt(q, kt, vt, bias, suffix_upper, *, tq):
    bsz, t, sb_dim = q.shape
    n_pairs = sb_dim // LANES
    kb = suffix_upper.shape[0]
    qspec = pl.BlockSpec((None, tq, LANES), lambda b, hp, i: (b, i, hp))
    kspec = pl.BlockSpec((None, 2, HEAD_DIM, t), lambda b, hp, i: (b, hp, 0, 0))
    return pl.pallas_call(
        functools.partial(_attn_prompt_body, tq=tq, kb=kb),
        grid=(bsz, n_pairs, t // tq),
        in_specs=[pl.BlockSpec(memory_space=pltpu.SMEM), qspec, kspec, kspec,
                  pl.BlockSpec(suffix_upper.shape, lambda b, hp, i: (0, 0))],
        out_specs=qspec,
        out_shape=jax.ShapeDtypeStruct((bsz, t, sb_dim), BF16),
        scratch_shapes=[pltpu.VMEM((2, tq, LANES), F32), pltpu.VMEM((tq, LANES), F32),
                        pltpu.VMEM((tq, 2 * LANES), BF16)],
        compiler_params=_cparams(3),
        name="attn_prompt",
    )(bias, q, kt, vt, suffix_upper)


def _attn_sample_body(pt_ref, bias_ref, q_ref, kn_ref, vn_ref, *rest, n_heads, dec_seq, n_pages, seqs):
    del pt_ref
    so_ref, o_ref = rest[2 * seqs * n_pages:]
    biasb = jnp.concatenate([jnp.full((dec_seq, LANES), bias_ref[h], F32) for h in range(n_heads)], axis=0)
    for s in range(seqs):
        kpages = rest[s * n_pages:(s + 1) * n_pages]
        vpages = rest[(seqs + s) * n_pages:(seqs + s + 1) * n_pages]
        o_ref[s] = _attn_sample_one(q_ref[s], kn_ref[s], vn_ref[s], kpages, vpages, so_ref[...], biasb,
                                    n_heads=n_heads, dec_seq=dec_seq)


def _attn_sample_one(q, k_new, v_new, kpage_refs, vpage_refs, suffix_ones, biasb, *, n_heads, dec_seq):
    n_pages = len(kpage_refs)
    rows = n_heads * dec_seq
    sb_dim = n_heads * HEAD_DIM
    lane = lax.broadcasted_iota(jnp.int32, (dec_seq, sb_dim), 1)
    in_head = [(lane >= h * HEAD_DIM) & (lane < (h + 1) * HEAD_DIM) for h in range(n_heads)]
    qbd = jnp.concatenate([jnp.where(in_head[h], q, 0.0) for h in range(n_heads)], axis=0).astype(BF16)
    pad = jnp.zeros((LANES - dec_seq, sb_dim), F32)
    kn = jnp.concatenate([k_new, pad], axis=0).astype(BF16)
    vn = jnp.concatenate([v_new, pad], axis=0).astype(BF16)
    kidx = lax.broadcasted_iota(jnp.int32, (rows, LANES), 1)
    trow = lax.broadcasted_iota(jnp.int32, (rows, LANES), 0) % dec_seq
    new_mask = kidx < trow
    pages = list(reversed(range(n_pages)))
    z = jnp.concatenate([_dot_nt(qbd, kn) + biasb]
                        + [_dot(qbd, kpage_refs[pg][...].astype(BF16)) + biasb for pg in pages], axis=0)
    log_beta, log_rem = _sb_logs(z)
    log_rem = jnp.concatenate([jnp.where(new_mask, log_rem[:rows], 0.0), log_rem[rows:]], axis=0)
    cs = _dot(log_rem.astype(BF16), suffix_ones)
    carries = [jnp.zeros((rows, LANES), F32)]
    for p in range(n_pages):
        carries.append(carries[-1] + cs[p * rows:(p + 1) * rows, LANES:])
    att = jnp.exp2(log_beta + cs[:, :LANES] + jnp.concatenate(carries, axis=0))
    acc = _dot(jnp.where(new_mask, att[:rows], 0.0).astype(BF16), vn)
    for p, pg in enumerate(pages):
        acc = acc + _dot_nt(att[(p + 1) * rows:(p + 2) * rows].astype(BF16), vpage_refs[pg][...].astype(BF16))
    out = jnp.zeros((dec_seq, sb_dim), F32)
    for h in range(n_heads):
        out = jnp.where(in_head[h], acc[h * dec_seq:(h + 1) * dec_seq, :], out)
    return out


def _attn_sample(page_table, bias, q, k_new, v_new, cache_kt, cache_vt, suffix_ones, *, layer):
    nseq, dec_seq, sb_dim = q.shape
    n_heads = sb_dim // HEAD_DIM
    n_pages = page_table.shape[1]
    page_size = cache_kt.shape[-1]
    assert page_size == LANES
    seqs = 2 if nseq % 2 == 0 else 1
    seq_spec = pl.BlockSpec((seqs, dec_seq, sb_dim), lambda n, pt: (n, 0, 0))

    def page_spec(s, pg):
        return pl.BlockSpec((None, None, sb_dim, page_size), lambda n, pt: (layer, pt[n * seqs + s, pg], 0, 0))

    page_specs = [page_spec(s, pg) for s in range(seqs) for pg in range(n_pages)]
    grid_spec = pltpu.PrefetchScalarGridSpec(
        num_scalar_prefetch=1,
        grid=(nseq // seqs,),
        in_specs=[pl.BlockSpec(memory_space=pltpu.SMEM), seq_spec, seq_spec, seq_spec]
        + page_specs + page_specs
        + [pl.BlockSpec(suffix_ones.shape, lambda n, pt: (0, 0))],
        out_specs=seq_spec,
    )
    return pl.pallas_call(
        functools.partial(_attn_sample_body, n_heads=n_heads, dec_seq=dec_seq, n_pages=n_pages, seqs=seqs),
        grid_spec=grid_spec,
        out_shape=jax.ShapeDtypeStruct((nseq, dec_seq, sb_dim), F32),
        compiler_params=_cparams(1),
        name="attn_sample",
    )(page_table, bias, q, k_new, v_new, *([cache_kt] * (seqs * n_pages)), *([cache_vt] * (seqs * n_pages)),
      suffix_ones)


def _merge_ffn_body(x_ref, a_ref, b_ref, c_ref, g_ref, wpa_ref, wpb_ref, wpc_ref, wo_ref, n2_ref,
                    wg_ref, wu_ref, wd_ref, o_ref, *, d_model, ff_chunk):
    merged = (g_ref[:, 0:d_model].astype(F32) * _dot(a_ref[...], wpa_ref[...])
              + g_ref[:, d_model:2 * d_model].astype(F32) * _dot(b_ref[...], wpb_ref[...])
              + g_ref[:, 2 * d_model:3 * d_model].astype(F32) * _dot(c_ref[...], wpc_ref[...]))
    acc = x_ref[...] + _dot(merged.astype(BF16), wo_ref[...])
    h = _rms_rows(acc, n2_ref[...]).astype(BF16)
    for c0 in range(0, wg_ref.shape[1], ff_chunk):
        gate = _dot(h, wg_ref[:, c0:c0 + ff_chunk])
        up = _dot(h, wu_ref[:, c0:c0 + ff_chunk])
        act = (gate * _sigmoid(gate) * up).astype(BF16)
        acc = acc + _dot(act, wd_ref[c0:c0 + ff_chunk, :])
    o_ref[...] = acc


def _merge_ffn(x, a_out, b_out, c_out, gates, w_pa, w_pb, w_pc, w_o, norm2, w_gate, w_up, w_down,
               *, layer, tm, ff_chunk):
    n, d_model = x.shape
    row = lambda arr: pl.BlockSpec((tm, arr.shape[1]), lambda i: (i, 0))
    full = lambda arr: _fixed_spec(arr, layer if arr.ndim == 3 else None, pipeline_mode=pl.Buffered(1))
    rows = (x, a_out, b_out, c_out, gates)
    consts = (w_pa, w_pb, w_pc, w_o, norm2, w_gate, w_up, w_down)
    return pl.pallas_call(
        functools.partial(_merge_ffn_body, d_model=d_model, ff_chunk=ff_chunk),
        grid=(n // tm,),
        in_specs=[row(r) for r in rows] + [full(c) for c in consts],
        out_specs=row(x),
        out_shape=jax.ShapeDtypeStruct((n, d_model), F32),
        compiler_params=_cparams(1),
        name="merge_ffn",
    )(*rows, *consts)


def _row_tile(n, target):
    tm = min(n, target)
    assert n % tm == 0
    return tm


def _ff_chunk(d_ff):
    for c in (512, 256, 128):
        if d_ff % c == 0:
            return c
    return d_ff


def kernel(x_prompt, x_sample, cache_k, cache_v, state_pool, page_table, norm1, w_in, w_pool, pool_scale,
           w_s, b_s, q_gain, k_gain, sb_bias, w_pa, w_pb, w_pc, w_o, norm2, w_gate, w_up, w_down):
    depth = w_in.shape[0]
    bsz, seq, d_model = x_prompt.shape
    nseq, dec_seq, _ = x_sample.shape
    n_pool, page_size, n_heads, head_dim = cache_k.shape[1:]
    n_pages = page_table.shape[1]
    past_len = n_pages * page_size
    sb_dim = n_heads * head_dim
    d_ff = w_gate.shape[2]
    assert head_dim == HEAD_DIM and w_pool.shape[1] == N_GROUPS and w_s.shape[2] == CHUNK
    assert seq % CHUNK == 0 and past_len % CHUNK == 0 and dec_seq <= CHUNK

    lane = jnp.arange(sb_dim)
    hmean = jnp.where((lane[:, None] // HEAD_DIM) == (lane[None, :] // HEAD_DIM), 1.0 / HEAD_DIM, 0.0).astype(BF16)
    kidx = jnp.arange(LANES)
    suffix_ones = jnp.concatenate(
        [(kidx[:, None] > kidx[None, :]).astype(BF16), jnp.ones((LANES, LANES), BF16)], axis=1)
    kb_p = min(PROMPT_KEY_BLOCK, seq)
    kidx_p = jnp.arange(kb_p)
    suffix_upper = (kidx_p[:, None] > kidx_p[None, :]).astype(BF16)
    w_in_b = w_in.astype(BF16)
    w_pa_b, w_pb_b, w_pc_b, w_o_b = (w.astype(BF16) for w in (w_pa, w_pb, w_pc, w_o))
    w_gate_b, w_up_b, w_down_b = (w.astype(BF16) for w in (w_gate, w_up, w_down))
    w_s_b = w_s.astype(BF16)
    grp = jnp.arange(MIX_DIM) // GROUP_DIM
    wpool_bd = jnp.where(grp[None, :, None] == grp[None, None, :],
                         jnp.tile(w_pool.reshape(depth, MIX_DIM, GROUP_DIM), (1, 1, N_GROUPS)), 0.0).astype(BF16)
    bsf = jnp.repeat(jnp.swapaxes(b_s, 1, 2), GROUP_DIM, axis=2)
    wrow = jnp.repeat(jnp.transpose(w_s[:, :, :dec_seq, :dec_seq], (0, 2, 3, 1)), GROUP_DIM, axis=3)
    qg = jnp.tile(q_gain, (1, n_heads)) * (HEAD_DIM ** -0.5 * LOG2_E)
    bias2 = sb_bias * LOG2_E
    kg = jnp.tile(k_gain, (1, n_heads))
    cache_kt = jnp.transpose(cache_k, (0, 1, 3, 4, 2)).reshape(depth, n_pool, sb_dim, page_size)
    cache_vt = jnp.transpose(cache_v, (0, 1, 3, 4, 2)).reshape(depth, n_pool, sb_dim, page_size)

    n_p = bsz * seq
    n_s = nseq * dec_seq
    tm_p = _row_tile(n_p, 512)
    tm_s = _row_tile(n_s, 512)
    tq = _row_tile(seq, 512)
    ff_chunk = _ff_chunk(d_ff)

    xp = x_prompt.reshape(n_p, d_model)
    xs = jnp.swapaxes(x_sample, 0, 1).reshape(n_s, d_model)
    outs = [[] for _ in range(7)]
    for l in range(depth):
        lw = dict(norm1=norm1[l][None], w_in=w_in_b, qg=qg[l][None], kg=kg[l][None])
        a, u, vb, q, kt, vt, gates = _in_proj(
            xp, lw["norm1"], lw["w_in"], lw["qg"], lw["kg"], hmean, layer=l, tm=tm_p, seq_len=seq)
        a3 = a.reshape(bsz, seq, MIX_DIM)
        a_out, b_out = _mix_prompt(a3, u.reshape(bsz, seq, MIX_DIM), vb.reshape(bsz, seq, MIX_DIM),
                                   w_s_b[l], bsf[l], wpool_bd[l], pool_scale[l][None], rows=tq)
        c_out = _attn_prompt(q.reshape(bsz, seq, sb_dim), kt, vt, bias2[l], suffix_upper, tq=tq)
        xp = _merge_ffn(xp, a_out.reshape(n_p, MIX_DIM), b_out.reshape(n_p, MIX_DIM), c_out.reshape(n_p, sb_dim),
                        gates, w_pa_b, w_pb_b, w_pc_b, w_o_b, norm2[l][None],
                        w_gate_b, w_up_b, w_down_b, layer=l, tm=tm_p, ff_chunk=ff_chunk)
        outs[0].append(kt)
        outs[1].append(vt)
        outs[2].append(a3[:, seq - POOL_BUF:, :])
        a, u, vb, q, k32, v32, gates = _in_proj(
            xs, lw["norm1"], lw["w_in"], lw["qg"], lw["kg"], hmean, layer=l, tm=tm_s)
        tmaj = lambda arr: arr.reshape(dec_seq, nseq, arr.shape[-1])
        smaj = lambda arr: jnp.swapaxes(tmaj(arr), 0, 1)
        a_out, b_out, new_pool = _mix_sample(
            tmaj(a), jnp.swapaxes(state_pool[l], 0, 1), tmaj(u), tmaj(vb),
            wrow[l], bsf[l], wpool_bd[l], pool_scale[l][None], first_pos=past_len)
        k_new = smaj(k32)
        v_new = smaj(v32)
        c_out = _attn_sample(page_table, bias2[l], smaj(q.astype(F32)), k_new, v_new,
                             cache_kt, cache_vt, suffix_ones, layer=l)
        c_out = jnp.swapaxes(c_out, 0, 1).reshape(n_s, sb_dim).astype(BF16)
        xs = _merge_ffn(xs, a_out.reshape(n_s, MIX_DIM), b_out.reshape(n_s, MIX_DIM), c_out,
                        gates, w_pa_b, w_pb_b, w_pc_b, w_o_b, norm2[l][None],
                        w_gate_b, w_up_b, w_down_b, layer=l, tm=tm_s, ff_chunk=ff_chunk)
        outs[3].append(k_new.reshape(nseq, dec_seq, n_heads, head_dim))
        outs[4].append(v_new.reshape(nseq, dec_seq, n_heads, head_dim))
        outs[5].append(jnp.swapaxes(new_pool, 0, 1))
        outs[6].append(smaj(vb))
    y_prompt = xp.reshape(bsz, seq, d_model)
    y_sample = jnp.swapaxes(xs.reshape(dec_seq, nseq, d_model), 0, 1)
    stacked = [jnp.stack(o) for o in outs]
    stacked[0] = jnp.transpose(stacked[0], (0, 1, 4, 2, 3))
    stacked[1] = jnp.transpose(stacked[1], (0, 1, 4, 2, 3))
    return (y_prompt, y_sample) + tuple(stacked)
```

```python
import functools

import jax
import jax.numpy as jnp
from jax import lax
from jax.experimental import pallas as pl
from jax.experimental.pallas import tpu as pltpu

F32 = jnp.float32
BF16 = jnp.bfloat16

NORM_EPS = 1e-6
POOL_WINDOWS = (2, 4, 8, 16)
POOL_BUF = 15
GROUP_DIM = 64
N_GROUPS = 4
MIX_DIM = N_GROUPS * GROUP_DIM
CHUNK = 128
HEAD_DIM = 64
LANES = 128
LOG2_E = 1.4426950408889634
PROMPT_KEY_BLOCK = 256
UNROLL_KEY_BLOCKS = 4
VMEM_LIMIT_BYTES = 56 * 1024 * 1024


def _cparams(n_grid_dims):
    return pltpu.CompilerParams(
        dimension_semantics=("arbitrary",) * n_grid_dims,
        vmem_limit_bytes=VMEM_LIMIT_BYTES)


def _fixed_spec(arr, layer=None, **kwargs):
    if layer is None:
        return pl.BlockSpec(arr.shape, lambda i: (0,) * arr.ndim, **kwargs)
    return pl.BlockSpec((None,) + arr.shape[1:], lambda i: (layer,) + (0,) * (arr.ndim - 1), **kwargs)


def _dot(a, b):
    return jnp.dot(a, b, preferred_element_type=F32)


def _dot_nt(a, b):
    return lax.dot_general(a, b, (((1,), (1,)), ((), ())), preferred_element_type=F32)


def _rms_rows(x, gain):
    ms = jnp.mean(x * x, axis=-1, keepdims=True)
    return (x * lax.rsqrt(ms + NORM_EPS)) * gain


def _sigmoid(x):
    return 1.0 / (1.0 + jnp.exp(-x))


def _in_proj_body(x_ref, n1_ref, w_ref, qg_ref, kg_ref, hmean_ref,
                  a_ref, u_ref, vb_ref, q_ref, k_ref, v_ref, g_ref,
                  *, sb_dim, d_model, transpose_kv):
    hb = _rms_rows(x_ref[...], n1_ref[...]).astype(BF16)

    def proj(c0, width):
        return _dot(hb, w_ref[:, c0:c0 + width])

    def head_rms(y, gain):
        ms = _dot((y * y).astype(BF16), hmean_ref[...])
        return (y * lax.rsqrt(ms + NORM_EPS)) * gain

    def store_kv(ref, y):
        if transpose_kv:
            ref[...] = jnp.transpose(y).reshape(ref.shape)
        else:
            ref[...] = y

    c = 0
    a_ref[...] = proj(c, MIX_DIM)
    c += MIX_DIM
    u_ref[...] = proj(c, MIX_DIM)
    c += MIX_DIM
    vb_ref[...] = proj(c, MIX_DIM)
    c += MIX_DIM
    q_ref[...] = head_rms(proj(c, sb_dim), qg_ref[...]).astype(BF16)
    c += sb_dim
    store_kv(k_ref, head_rms(proj(c, sb_dim), kg_ref[...]))
    c += sb_dim
    store_kv(v_ref, proj(c, sb_dim))
    c += sb_dim
    for j in range(3):
        g_ref[:, j * d_model:(j + 1) * d_model] = _sigmoid(proj(c, d_model)).astype(BF16)
        c += d_model


def _in_proj(x, norm1, w_in_all, q_gain_s, k_gain_t, hmean, *, layer, tm, seq_len=None):
    n, d_model = x.shape
    sb_dim = hmean.shape[0]
    row = lambda width: pl.BlockSpec((tm, width), lambda i: (i, 0))
    full = _fixed_spec
    if seq_len is None:
        kv_shape = jax.ShapeDtypeStruct((n, sb_dim), F32)
        kv_spec = row(sb_dim)
    else:
        tiles = seq_len // tm
        n_heads = sb_dim // HEAD_DIM
        kv_shape = jax.ShapeDtypeStruct((n // seq_len, n_heads, HEAD_DIM, seq_len), F32)
        kv_spec = pl.BlockSpec((None, n_heads, HEAD_DIM, tm), lambda i: (i // tiles, 0, 0, i % tiles))
    out_shapes = (
        jax.ShapeDtypeStruct((n, MIX_DIM), F32),
        jax.ShapeDtypeStruct((n, MIX_DIM), F32),
        jax.ShapeDtypeStruct((n, MIX_DIM), F32),
        jax.ShapeDtypeStruct((n, sb_dim), BF16),
        kv_shape,
        kv_shape,
        jax.ShapeDtypeStruct((n, 3 * d_model), BF16),
    )
    out_specs = [row(MIX_DIM)] * 3 + [row(sb_dim), kv_spec, kv_spec, row(3 * d_model)]
    return pl.pallas_call(
        functools.partial(_in_proj_body, sb_dim=sb_dim, d_model=d_model, transpose_kv=seq_len is not None),
        grid=(n // tm,),
        in_specs=[row(d_model), full(norm1), full(w_in_all, layer), full(q_gain_s), full(k_gain_t), full(hmean)],
        out_specs=out_specs,
        out_shape=out_shapes,
        compiler_params=_cparams(1),
        name="in_proj",
    )(x, norm1, w_in_all, q_gain_s, k_gain_t, hmean)


def _group_of_lane(shape):
    return lax.broadcasted_iota(jnp.int32, shape, len(shape) - 1) // GROUP_DIM


def _mix_prompt_rows(c, a_ref, halo_ref, u_ref, vb_ref, ws_ref, bsf_ref, wpool_ref, pscale_ref,
                     e_ref, s2_ref, s4_ref, s8_ref):
    rows = a_ref.shape[0]
    end = 32 + rows
    a = a_ref[...]
    e_ref[0:16, :] = jnp.zeros((16, MIX_DIM), F32)
    e_ref[16:32, :] = jnp.where(c > 0, halo_ref[...], 0.0)
    e_ref[32:end, :] = a
    s2_ref[8:end, :] = e_ref[8:end, :] + e_ref[7:end - 1, :]
    s4_ref[16:end, :] = s2_ref[16:end, :] + s2_ref[14:end - 2, :]
    s8_ref[24:end, :] = s4_ref[24:end, :] + s4_ref[20:end - 4, :]
    s16 = s8_ref[32:end, :] + s8_ref[24:end - 8, :]
    grp = _group_of_lane((rows, MIX_DIM))
    row = lax.broadcasted_iota(jnp.int32, (rows, MIX_DIM), 0)
    s = jnp.where(grp == 0, s2_ref[32:end, :],
                  jnp.where(grp == 1, s4_ref[32:end, :],
                            jnp.where(grp == 2, s8_ref[32:end, :], s16)))
    window = jnp.left_shift(2, grp)
    cnt = jnp.minimum(c * rows + row + 1, window).astype(F32)
    d = s / cnt - a
    a_out = (_dot(d.astype(BF16), wpool_ref[...]) * pscale_ref[...]).astype(BF16)
    r128 = lax.broadcasted_iota(jnp.int32, (CHUNK, CHUNK), 0)
    c128 = lax.broadcasted_iota(jnp.int32, (CHUNK, CHUNK), 1)
    wg = [jnp.where(c128 <= r128, ws_ref[g], jnp.zeros((), BF16)) for g in range(N_GROUPS)]
    grp_c = _group_of_lane((CHUNK, MIX_DIM))
    b_chunks = []
    for r0 in range(0, rows, CHUNK):
        vbb = vb_ref[r0:r0 + CHUNK, :].astype(BF16)
        o = jnp.zeros((CHUNK, MIX_DIM), F32)
        for g in range(N_GROUPS):
            o = jnp.where(grp_c == g, _dot(wg[g], vbb), o)
        b_chunks.append((u_ref[r0:r0 + CHUNK, :] * (o + bsf_ref[...])).astype(BF16))
    return a_out, jnp.concatenate(b_chunks, axis=0)


def _mix_sample_body(a_ref, prev_ref, u_ref, vb_ref, wrow_ref, bsf_ref, wpool_ref, pscale_ref,
                     ao_ref, bo_ref, np_ref, *, dec_seq, first_pos):
    nseq = a_ref.shape[1]
    grp = _group_of_lane((nseq, MIX_DIM))
    e = [prev_ref[j] for j in range(POOL_BUF)] + [a_ref[t] for t in range(dec_seq)]
    n_e = len(e)
    sums = []
    prev_level, span = e, 1
    for _ in POOL_WINDOWS:
        level = [None] * n_e
        for j in range(n_e):
            if j - span >= 0 and prev_level[j] is not None and prev_level[j - span] is not None:
                level[j] = prev_level[j] + prev_level[j - span]
        sums.append(level)
        prev_level, span = level, span * 2
    for t in range(dec_seq):
        j = POOL_BUF + t
        mean = None
        for k, w in enumerate(POOL_WINDOWS):
            m = sums[k][j] / float(min(first_pos + t + 1, w))
            mean = m if mean is None else jnp.where(grp == k, m, mean)
        d = mean - e[j]
        ao_ref[t] = (_dot(d.astype(BF16), wpool_ref[...]) * pscale_ref[...]).astype(BF16)
        o = bsf_ref[t:t + 1, :]
        for s in range(t + 1):
            o = o + wrow_ref[t, s:s + 1, :] * vb_ref[s]
        bo_ref[t] = (u_ref[t] * o).astype(BF16)
    for j in range(POOL_BUF):
        np_ref[j] = e[n_e - POOL_BUF + j]


def _mix_sample(a, prev, u, vb, wrow, bsf, wpool_bd, pscale, *, first_pos):
    dec_seq, nseq, _ = a.shape
    full = lambda arr: pl.BlockSpec(arr.shape, lambda i: (0,) * arr.ndim)
    args = (a, prev, u, vb, wrow, bsf, wpool_bd, pscale)
    out_shapes = (jax.ShapeDtypeStruct((dec_seq, nseq, MIX_DIM), BF16),
                  jax.ShapeDtypeStruct((dec_seq, nseq, MIX_DIM), BF16),
                  jax.ShapeDtypeStruct((POOL_BUF, nseq, MIX_DIM), F32))
    return pl.pallas_call(
        functools.partial(_mix_sample_body, dec_seq=dec_seq, first_pos=first_pos),
        grid=(1,),
        in_specs=[full(x) for x in args],
        out_specs=[full(s) for s in out_shapes],
        out_shape=out_shapes,
        compiler_params=_cparams(1),
        name="mix_sample",
    )(*args)


def _sb_logs(z):
    neg_part = jnp.minimum(z, 0.0)
    neg_rest = neg_part - z
    softplus = jnp.log2(1.0 + jnp.exp2(neg_part + neg_rest))
    return neg_part - softplus, neg_rest - softplus


def _attn_prompt_body(bias_ref, q_ref, k_ref, v_ref, su_ref, o_ref, carry_ref, acc_ref, qa_ref, *, tq, kb):
    hp = pl.program_id(1)
    i = pl.program_id(2)
    n_diag = tq // kb
    su = su_ref[...]
    zeros = jnp.zeros((HEAD_DIM, kb), BF16)
    carry_ref[...] = jnp.zeros(carry_ref.shape, F32)
    acc_ref[...] = jnp.zeros(acc_ref.shape, F32)
    lane = lax.broadcasted_iota(jnp.int32, (1, 2 * kb), 1)
    bias_row = jnp.where(lane < kb, bias_ref[hp * 2], bias_ref[hp * 2 + 1])
    b_hi = bias_row.astype(BF16).astype(F32)
    b_mid = (bias_row - b_hi).astype(BF16).astype(F32)
    b_lo = (bias_row - b_hi) - b_mid
    chan = lax.broadcasted_iota(jnp.int32, (LANES, 2 * kb), 0)
    bias_rows = jnp.where(chan == 0, b_hi, jnp.where(chan == 1, b_mid, jnp.where(chan == 2, b_lo, 0.0))).astype(BF16)
    qlane = lax.broadcasted_iota(jnp.int32, (tq, LANES), 1)
    qa_ref[:, 0:LANES] = q_ref[...]
    qa_ref[:, LANES:2 * LANES] = jnp.where(qlane < 3, 1.0, 0.0).astype(BF16)

    def block_diag(ref, off):
        t = ref[:, :, pl.ds(off, kb)].astype(BF16)
        return jnp.concatenate([jnp.concatenate([t[0], zeros], axis=1),
                                jnp.concatenate([zeros, t[1]], axis=1)], axis=0)

    def step(j, row0, row1, mask):
        off = pl.multiple_of(j * kb, kb)
        z = _dot(qa_ref[row0:row1, :], jnp.concatenate([block_diag(k_ref, off), bias_rows], axis=0))
        log_beta, log_rem = _sb_logs(z)
        if mask is not None:
            log_rem = jnp.where(mask, log_rem, 0.0)
        carries, suffix = [], []
        for hh in range(2):
            lr = log_rem[:, hh * kb:(hh + 1) * kb]
            suffix.append(_dot(lr.astype(BF16), su))
            carry = carry_ref[hh, row0:row1, :]
            carries.append(jnp.concatenate([carry] * (kb // LANES), axis=1))
            carry_ref[hh, row0:row1, :] = carry + jnp.sum(lr, axis=-1, keepdims=True)
        att = jnp.exp2(log_beta + jnp.concatenate(suffix, axis=1) + jnp.concatenate(carries, axis=1))
        if mask is not None:
            att = jnp.where(mask, att, 0.0)
        acc_ref[row0:row1, :] += _dot_nt(att.astype(BF16), block_diag(v_ref, off))

    first = i * n_diag
    key_in_block = lax.broadcasted_iota(jnp.int32, (kb, 2 * kb), 1) % kb
    triangle = key_in_block < lax.broadcasted_iota(jnp.int32, (kb, 2 * kb), 0)
    for dj in reversed(range(n_diag)):
        step(first + dj, dj * kb, (dj + 1) * kb, triangle)
        if (dj + 1) * kb < tq:
            step(first + dj, (dj + 1) * kb, tq, None)

    def run(top, n_blocks, unroll):
        def body(t, c):
            for dj in range(unroll):
                step(top - 1 - dj - unroll * t, 0, tq, None)
            return c
        lax.fori_loop(0, n_blocks // unroll, body, 0)

    n_main = (first // UNROLL_KEY_BLOCKS) * UNROLL_KEY_BLOCKS
    run(first, n_main, UNROLL_KEY_BLOCKS)
    if UNROLL_KEY_BLOCKS != n_diag:
        assert UNROLL_KEY_BLOCKS % n_diag == 0
        run(first - n_main, first - n_main, n_diag)
    o_ref[...] = acc_ref[...].astype(o_ref.dtype)


def _attn_prompt(q, kt, vt, bias, suffix_upper, *, tq):
    bsz, t, sb_dim = q.shape
    n_pairs = sb_dim // LANES
    kb = suffix_upper.shape[0]
    qspec = pl.BlockSpec((None, tq, LANES), lambda b, hp, i: (b, i, hp))
    kspec = pl.BlockSpec((None, 2, HEAD_DIM, t), lambda b, hp, i: (b, hp, 0, 0))
    return pl.pallas_call(
        functools.partial(_attn_prompt_body, tq=tq, kb=kb),
        grid=(bsz, n_pairs, t // tq),
        in_specs=[pl.BlockSpec(memory_space=pltpu.SMEM), qspec, kspec, kspec,
                  pl.BlockSpec(suffix_upper.shape, lambda b, hp, i: (0, 0))],
        out_specs=qspec,
        out_shape=jax.ShapeDtypeStruct((bsz, t, sb_dim), BF16),
        scratch_shapes=[pltpu.VMEM((2, tq, LANES), F32), pltpu.VMEM((tq, LANES), F32),
                        pltpu.VMEM((tq, 2 * LANES), BF16)],
        compiler_params=_cparams(3),
        name="attn_prompt",
    )(bias, q, kt, vt, suffix_upper)


def _attn_sample_body(pt_ref, bias_ref, q_ref, kn_ref, vn_ref, *rest, n_heads, dec_seq, n_pages, seqs):
    del pt_ref
    so_ref, o_ref = rest[2 * seqs * n_pages:]
    biasb = jnp.concatenate([jnp.full((dec_seq, LANES), bias_ref[h], F32) for h in range(n_heads)], axis=0)
    for s in range(seqs):
        kpages = rest[s * n_pages:(s + 1) * n_pages]
        vpages = rest[(seqs + s) * n_pages:(seqs + s + 1) * n_pages]
        o_ref[s] = _attn_sample_one(q_ref[s], kn_ref[s], vn_ref[s], kpages, vpages, so_ref[...], biasb,
                                    n_heads=n_heads, dec_seq=dec_seq)


def _attn_sample_one(q, k_new, v_new, kpage_refs, vpage_refs, suffix_ones, biasb, *, n_heads, dec_seq):
    n_pages = len(kpage_refs)
    rows = n_heads * dec_seq
    sb_dim = n_heads * HEAD_DIM
    lane = lax.broadcasted_iota(jnp.int32, (dec_seq, sb_dim), 1)
    in_head = [(lane >= h * HEAD_DIM) & (lane < (h + 1) * HEAD_DIM) for h in range(n_heads)]
    qbd = jnp.concatenate([jnp.where(in_head[h], q, 0.0) for h in range(n_heads)], axis=0).astype(BF16)
    pad = jnp.zeros((LANES - dec_seq, sb_dim), F32)
    kn = jnp.concatenate([k_new, pad], axis=0).astype(BF16)
    vn = jnp.concatenate([v_new, pad], axis=0).astype(BF16)
    kidx = lax.broadcasted_iota(jnp.int32, (rows, LANES), 1)
    trow = lax.broadcasted_iota(jnp.int32, (rows, LANES), 0) % dec_seq
    new_mask = kidx < trow
    pages = list(reversed(range(n_pages)))
    z = jnp.concatenate([_dot_nt(qbd, kn) + biasb]
                        + [_dot(qbd, kpage_refs[pg][...].astype(BF16)) + biasb for pg in pages], axis=0)
    log_beta, log_rem = _sb_logs(z)
    log_rem = jnp.concatenate([jnp.where(new_mask, log_rem[:rows], 0.0), log_rem[rows:]], axis=0)
    cs = _dot(log_rem.astype(BF16), suffix_ones)
    carries = [jnp.zeros((rows, LANES), F32)]
    for p in range(n_pages):
        carries.append(carries[-1] + cs[p * rows:(p + 1) * rows, LANES:])
    att = jnp.exp2(log_beta + cs[:, :LANES] + jnp.concatenate(carries, axis=0))
    acc = _dot(jnp.where(new_mask, att[:rows], 0.0).astype(BF16), vn)
    for p, pg in enumerate(pages):
        acc = acc + _dot_nt(att[(p + 1) * rows:(p + 2) * rows].astype(BF16), vpage_refs[pg][...].astype(BF16))
    out = jnp.zeros((dec_seq, sb_dim), F32)
    for h in range(n_heads):
        out = jnp.where(in_head[h], acc[h * dec_seq:(h + 1) * dec_seq, :], out)
    return out


def _attn_sample(page_table, bias, q, k_new, v_new, cache_kt, cache_vt, suffix_ones, *, layer):
    nseq, dec_seq, sb_dim = q.shape
    n_heads = sb_dim // HEAD_DIM
    n_pages = page_table.shape[1]
    page_size = cache_kt.shape[-1]
    assert page_size == LANES
    seqs = 2 if nseq % 2 == 0 else 1
    seq_spec = pl.BlockSpec((seqs, dec_seq, sb_dim), lambda n, pt: (n, 0, 0))

    def page_spec(s, pg):
        return pl.BlockSpec((None, None, sb_dim, page_size), lambda n, pt: (layer, pt[n * seqs + s, pg], 0, 0))

    page_specs = [page_spec(s, pg) for s in range(seqs) for pg in range(n_pages)]
    grid_spec = pltpu.PrefetchScalarGridSpec(
        num_scalar_prefetch=1,
        grid=(nseq // seqs,),
        in_specs=[pl.BlockSpec(memory_space=pltpu.SMEM), seq_spec, seq_spec, seq_spec]
        + page_specs + page_specs
        + [pl.BlockSpec(suffix_ones.shape, lambda n, pt: (0, 0))],
        out_specs=seq_spec,
    )
    return pl.pallas_call(
        functools.partial(_attn_sample_body, n_heads=n_heads, dec_seq=dec_seq, n_pages=n_pages, seqs=seqs),
        grid_spec=grid_spec,
        out_shape=jax.ShapeDtypeStruct((nseq, dec_seq, sb_dim), F32),
        compiler_params=_cparams(1),
        name="attn_sample",
    )(page_table, bias, q, k_new, v_new, *([cache_kt] * (seqs * n_pages)), *([cache_vt] * (seqs * n_pages)),
      suffix_ones)


def _merge_ffn_body(x_ref, *refs, d_model, ff_chunk, blocks_per_seq):
    if blocks_per_seq is None:
        a_ref, b_ref, *refs = refs
        a_out, b_out = a_ref[...], b_ref[...]
    else:
        mix_refs, refs, scratch = refs[:8], refs[8:-4], refs[-4:]
        a_out, b_out = _mix_prompt_rows(pl.program_id(0) % blocks_per_seq, *mix_refs, *scratch)
    c_ref, g_ref, wpa_ref, wpb_ref, wpc_ref, wo_ref, n2_ref, wg_ref, wu_ref, wd_ref, o_ref = refs
    merged = (g_ref[:, 0:d_model].astype(F32) * _dot(a_out, wpa_ref[...])
              + g_ref[:, d_model:2 * d_model].astype(F32) * _dot(b_out, wpb_ref[...])
              + g_ref[:, 2 * d_model:3 * d_model].astype(F32) * _dot(c_ref[...], wpc_ref[...]))
    acc = x_ref[...] + _dot(merged.astype(BF16), wo_ref[...])
    h = _rms_rows(acc, n2_ref[...]).astype(BF16)
    for c0 in range(0, wg_ref.shape[1], ff_chunk):
        gate = _dot(h, wg_ref[:, c0:c0 + ff_chunk])
        up = _dot(h, wu_ref[:, c0:c0 + ff_chunk])
        act = (gate * _sigmoid(gate) * up).astype(BF16)
        acc = acc + _dot(act, wd_ref[c0:c0 + ff_chunk, :])
    o_ref[...] = acc


def _merge_ffn(x, branches, c_out, gates, w_pa, w_pb, w_pc, w_o, norm2, w_gate, w_up, w_down,
               *, layer, tm, ff_chunk, seq_len=None):
    n, d_model = x.shape
    row = lambda arr: pl.BlockSpec((tm, arr.shape[1]), lambda i: (i, 0))
    full = lambda arr: _fixed_spec(arr, layer if arr.ndim >= 3 else None, pipeline_mode=pl.Buffered(1))
    consts = (w_pa, w_pb, w_pc, w_o, norm2, w_gate, w_up, w_down)
    if seq_len is None:
        args = (x, *branches, c_out, gates, *consts)
        in_specs = [row(x), row(branches[0]), row(branches[1]), row(c_out), row(gates)] + [full(c) for c in consts]
        scratch, blocks_per_seq = [], None
    else:
        a, u, vb, w_s, bsf, wpool_bd, pscale = branches
        assert tm % CHUNK == 0 and seq_len % tm == 0
        blocks_per_seq = seq_len // tm
        halo = pl.BlockSpec((16, MIX_DIM), lambda i: (jnp.maximum(i * (tm // 16) - 1, 0), 0))
        args = (x, a, a, u, vb, w_s, bsf, wpool_bd, pscale, c_out, gates, *consts)
        in_specs = ([row(x), row(a), halo, row(u), row(vb), full(w_s), full(bsf), full(wpool_bd), full(pscale),
                     row(c_out), row(gates)] + [full(c) for c in consts])
        scratch = [pltpu.VMEM((32 + tm, MIX_DIM), F32)] * 4
    return pl.pallas_call(
        functools.partial(_merge_ffn_body, d_model=d_model, ff_chunk=ff_chunk, blocks_per_seq=blocks_per_seq),
        grid=(n // tm,),
        in_specs=in_specs,
        out_specs=row(x),
        out_shape=jax.ShapeDtypeStruct((n, d_model), F32),
        scratch_shapes=scratch,
        compiler_params=_cparams(1),
        name="merge_ffn",
    )(*args)


def _row_tile(n, target):
    tm = min(n, target)
    assert n % tm == 0
    return tm


def _ff_chunk(d_ff):
    for c in (512, 256, 128):
        if d_ff % c == 0:
            return c
    return d_ff


def kernel(x_prompt, x_sample, cache_k, cache_v, state_pool, page_table, norm1, w_in, w_pool, pool_scale,
           w_s, b_s, q_gain, k_gain, sb_bias, w_pa, w_pb, w_pc, w_o, norm2, w_gate, w_up, w_down):
    depth = w_in.shape[0]
    bsz, seq, d_model = x_prompt.shape
    nseq, dec_seq, _ = x_sample.shape
    n_pool, page_size, n_heads, head_dim = cache_k.shape[1:]
    n_pages = page_table.shape[1]
    past_len = n_pages * page_size
    sb_dim = n_heads * head_dim
    d_ff = w_gate.shape[2]
    assert head_dim == HEAD_DIM and w_pool.shape[1] == N_GROUPS and w_s.shape[2] == CHUNK
    assert seq % CHUNK == 0 and past_len % CHUNK == 0 and dec_seq <= CHUNK

    lane = jnp.arange(sb_dim)
    hmean = jnp.where((lane[:, None] // HEAD_DIM) == (lane[None, :] // HEAD_DIM), 1.0 / HEAD_DIM, 0.0).astype(BF16)
    kidx = jnp.arange(LANES)
    suffix_ones = jnp.concatenate(
        [(kidx[:, None] > kidx[None, :]).astype(BF16), jnp.ones((LANES, LANES), BF16)], axis=1)
    kb_p = min(PROMPT_KEY_BLOCK, seq)
    kidx_p = jnp.arange(kb_p)
    suffix_upper = (kidx_p[:, None] > kidx_p[None, :]).astype(BF16)
    w_in_b = w_in.astype(BF16)
    w_pa_b, w_pb_b, w_pc_b, w_o_b = (w.astype(BF16) for w in (w_pa, w_pb, w_pc, w_o))
    w_gate_b, w_up_b, w_down_b = (w.astype(BF16) for w in (w_gate, w_up, w_down))
    w_s_b = w_s.astype(BF16)
    grp = jnp.arange(MIX_DIM) // GROUP_DIM
    wpool_bd = jnp.where(grp[None, :, None] == grp[None, None, :],
                         jnp.tile(w_pool.reshape(depth, MIX_DIM, GROUP_DIM), (1, 1, N_GROUPS)), 0.0).astype(BF16)
    bsf = jnp.repeat(jnp.swapaxes(b_s, 1, 2), GROUP_DIM, axis=2)
    wrow = jnp.repeat(jnp.transpose(w_s[:, :, :dec_seq, :dec_seq], (0, 2, 3, 1)), GROUP_DIM, axis=3)
    qg = jnp.tile(q_gain, (1, n_heads)) * (HEAD_DIM ** -0.5 * LOG2_E)
    bias2 = sb_bias * LOG2_E
    kg = jnp.tile(k_gain, (1, n_heads))
    cache_kt = jnp.transpose(cache_k, (0, 1, 3, 4, 2)).reshape(depth, n_pool, sb_dim, page_size)
    cache_vt = jnp.transpose(cache_v, (0, 1, 3, 4, 2)).reshape(depth, n_pool, sb_dim, page_size)

    n_p = bsz * seq
    n_s = nseq * dec_seq
    tm_p = _row_tile(n_p, 512)
    tm_s = _row_tile(n_s, 512)
    tq = _row_tile(seq, 512)
    ff_chunk = _ff_chunk(d_ff)

    xp = x_prompt.reshape(n_p, d_model)
    xs = jnp.swapaxes(x_sample, 0, 1).reshape(n_s, d_model)
    outs = [[] for _ in range(7)]
    for l in range(depth):
        lw = dict(norm1=norm1[l][None], w_in=w_in_b, qg=qg[l][None], kg=kg[l][None])
        a, u, vb, q, kt, vt, gates = _in_proj(
            xp, lw["norm1"], lw["w_in"], lw["qg"], lw["kg"], hmean, layer=l, tm=tm_p, seq_len=seq)
        a3 = a.reshape(bsz, seq, MIX_DIM)
        c_out = _attn_prompt(q.reshape(bsz, seq, sb_dim), kt, vt, bias2[l], suffix_upper, tq=tq)
        xp = _merge_ffn(xp, (a, u, vb, w_s_b, bsf, wpool_bd, pool_scale[l][None]), c_out.reshape(n_p, sb_dim),
                        gates, w_pa_b, w_pb_b, w_pc_b, w_o_b, norm2[l][None],
                        w_gate_b, w_up_b, w_down_b, layer=l, tm=tm_p, ff_chunk=ff_chunk, seq_len=seq)
        outs[0].append(kt)
        outs[1].append(vt)
        outs[2].append(a3[:, seq - POOL_BUF:, :])
        a, u, vb, q, k32, v32, gates = _in_proj(
            xs, lw["norm1"], lw["w_in"], lw["qg"], lw["kg"], hmean, layer=l, tm=tm_s)
        tmaj = lambda arr: arr.reshape(dec_seq, nseq, arr.shape[-1])
        smaj = lambda arr: jnp.swapaxes(tmaj(arr), 0, 1)
        a_out, b_out, new_pool = _mix_sample(
            tmaj(a), jnp.swapaxes(state_pool[l], 0, 1), tmaj(u), tmaj(vb),
            wrow[l], bsf[l], wpool_bd[l], pool_scale[l][None], first_pos=past_len)
        k_new = smaj(k32)
        v_new = smaj(v32)
        c_out = _attn_sample(page_table, bias2[l], smaj(q.astype(F32)), k_new, v_new,
                             cache_kt, cache_vt, suffix_ones, layer=l)
        c_out = jnp.swapaxes(c_out, 0, 1).reshape(n_s, sb_dim).astype(BF16)
        xs = _merge_ffn(xs, (a_out.reshape(n_s, MIX_DIM), b_out.reshape(n_s, MIX_DIM)), c_out,
                        gates, w_pa_b, w_pb_b, w_pc_b, w_o_b, norm2[l][None],
                        w_gate_b, w_up_b, w_down_b, layer=l, tm=tm_s, ff_chunk=ff_chunk)
        outs[3].append(k_new.reshape(nseq, dec_seq, n_heads, head_dim))
        outs[4].append(v_new.reshape(nseq, dec_seq, n_heads, head_dim))
        outs[5].append(jnp.swapaxes(new_pool, 0, 1))
        outs[6].append(smaj(vb))
    y_prompt = xp.reshape(bsz, seq, d_model)
    y_sample = jnp.swapaxes(xs.reshape(dec_seq, nseq, d_model), 0, 1)
    stacked = [jnp.stack(o) for o in outs]
    stacked[0] = jnp.transpose(stacked[0], (0, 1, 4, 2, 3))
    stacked[1] = jnp.transpose(stacked[1], (0, 1, 4, 2, 3))
    return (y_prompt, y_sample) + tuple(stacked)
```
